```python
import math
import jax, jax.numpy as jnp
from jax import lax
import numpy as np

D_MODEL = 1024
BATCH = 4
SEQ = 8192
DEPTH = 1

N_DIFF_HEADS = 4
DIFF_QK_DIM = 64
DIFF_V_DIM = 2 * DIFF_QK_DIM
DIFF_WIDTH = N_DIFF_HEADS * DIFF_V_DIM
N_RET_HEADS = 4
RET_QK_DIM = 64
RET_V_DIM = 2 * RET_QK_DIM
RET_WIDTH = N_RET_HEADS * RET_V_DIM
MIX_WIDTH = DIFF_WIDTH + RET_WIDTH
PROJ_SIZES = (
    N_DIFF_HEADS * 2 * DIFF_QK_DIM,
    N_DIFF_HEADS * 2 * DIFF_QK_DIM,
    DIFF_WIDTH,
    N_RET_HEADS * RET_QK_DIM,
    N_RET_HEADS * RET_QK_DIM,
    RET_WIDTH,
    RET_WIDTH,
)
PROJ_WIDTH = sum(PROJ_SIZES)
PROJ_SPLITS = tuple(int(s) for s in np.cumsum(PROJ_SIZES)[:-1])

ROPE_THETA = 10000.0
Q_BLOCK = 128
RET_CHUNK = 128
RMS_EPS = 1e-5

N_EXPERTS = 32
TOP_K = 4
D_FF_EXPERT = D_MODEL
SWIGLU_LIMIT = 7.0
SWIGLU_ALPHA = 1.702
MOE_BLOCK = 128

kernel_name = 'hybrid_diffattn_retention_moe'


def rms_norm(x, g):
    xf = x.astype(jnp.float32)
    y = xf * lax.rsqrt(jnp.mean(xf * xf, axis=-1, keepdims=True) + RMS_EPS)
    return (y * g.astype(jnp.float32)).astype(x.dtype)


def rope(x, positions):
    d = x.shape[-1]
    inv_freq = 1.0 / (ROPE_THETA ** (jnp.arange(0, d, 2, dtype=jnp.float32) / d))
    ang = positions.astype(jnp.float32)[..., None] * inv_freq
    ang = ang.reshape(ang.shape[:2] + (1,) * (x.ndim - 3) + (d // 2,))
    cos = jnp.concatenate([jnp.cos(ang), jnp.cos(ang)], axis=-1)
    sin = jnp.concatenate([jnp.sin(ang), jnp.sin(ang)], axis=-1)
    xf = x.astype(jnp.float32)
    x1, x2 = xf[..., : d // 2], xf[..., d // 2:]
    rot = jnp.concatenate([-x2, x1], axis=-1)
    return (xf * cos + rot * sin).astype(x.dtype)


def diff_attention(q, k, v, lam):
    B, S, H, _ = v.shape
    n_blocks = S // Q_BLOCK
    scale = DIFF_QK_DIM ** -0.5
    k_pos = jnp.arange(S)

    def block(i):
        start = i * Q_BLOCK
        qb = lax.dynamic_slice_in_dim(q, start, Q_BLOCK, axis=1)
        s = jnp.einsum('bqhmd,bkhmd->bhmqk', qb, k,
                       preferred_element_type=jnp.float32) * scale
        q_pos = start + jnp.arange(Q_BLOCK)
        causal = k_pos[None, :] <= q_pos[:, None]
        s = jnp.where(causal, s, -jnp.inf)
        p = jax.nn.softmax(s, axis=-1)
        a = p[:, :, 0] - lam * p[:, :, 1]
        return jnp.einsum('bhqk,bkhd->bqhd', a.astype(v.dtype), v)

    out = lax.map(block, jnp.arange(n_blocks))
    return jnp.moveaxis(out, 0, 1).reshape(B, S, H, v.shape[-1])


def retention(q, k, v, log_decay):
    B, S, H, dk = q.shape
    dv = v.shape[-1]
    C = RET_CHUNK
    nc = S // C
    q = q.reshape(B, nc, C, H, dk)
    k = (k * (dk ** -0.5)).reshape(B, nc, C, H, dk)
    v = v.reshape(B, nc, C, H, dv)
    idx = jnp.arange(C, dtype=jnp.float32)
    rel = idx[:, None] - idx[None, :]
    intra_decay = jnp.where(rel >= 0,
                            jnp.exp(log_decay[:, None, None] * jnp.maximum(rel, 0.0)),
                            0.0)
    scores = jnp.einsum('bnihd,bnjhd->bnhij', q, k,
                        preferred_element_type=jnp.float32) * intra_decay
    y_intra = jnp.einsum('bnhij,bnjhe->bnihe', scores, v.astype(jnp.float32))
    k_decay = jnp.exp(log_decay[None, :] * (C - 1 - idx)[:, None])
    kv = jnp.einsum('bnjhd,jh,bnjhe->bnhde', k.astype(jnp.float32), k_decay,
                    v.astype(jnp.float32))
    chunk_decay = jnp.exp(log_decay * C)[None, :, None, None]

    def step(state, kv_c):
        return chunk_decay * state + kv_c, state

    init = jnp.zeros((B, H, dk, dv), jnp.float32)
    _, prev = lax.scan(step, init, jnp.moveaxis(kv, 1, 0))
    prev = jnp.moveaxis(prev, 0, 1)
    q_decay = jnp.exp(log_decay[None, :] * (idx + 1.0)[:, None])
    y_cross = jnp.einsum('bnihd,ih,bnhde->bnihe', q.astype(jnp.float32), q_decay, prev)
    return (y_intra + y_cross).reshape(B, S, H, dv).astype(v.dtype)


def moe(x2d, w_router, b_router, w_in, b_in, w_out, b_out):
    N, D = x2d.shape
    logits = (x2d @ w_router).astype(jnp.float32) + b_router.astype(jnp.float32)
    top_val, top_idx = lax.top_k(logits, TOP_K)
    gates = jax.nn.softmax(top_val, axis=-1)
    NK = N * TOP_K
    e_flat = top_idx.reshape(NK)
    tok = jnp.arange(NK) // TOP_K
    order = jnp.argsort(e_flat)
    e_s = e_flat[order]
    tok_s = tok[order]
    g_s = gates.reshape(NK)[order]
    counts = jnp.bincount(e_flat, length=N_EXPERTS)
    offsets = jnp.cumsum(counts) - counts
    padded = ((counts + MOE_BLOCK - 1) // MOE_BLOCK) * MOE_BLOCK
    pad_end = jnp.cumsum(padded)
    pad_off = pad_end - padded
    dest = pad_off[e_s] + (jnp.arange(NK) - offsets[e_s])
    P = ((NK + MOE_BLOCK - 1) // MOE_BLOCK) * MOE_BLOCK + N_EXPERTS * MOE_BLOCK
    n_blk = P // MOE_BLOCK
    x_buf = jnp.zeros((P, D), x2d.dtype).at[dest].set(x2d[tok_s])
    blk_expert = jnp.minimum(
        jnp.searchsorted(pad_end, jnp.arange(n_blk) * MOE_BLOCK, side='right'),
        N_EXPERTS - 1)

    def expert_block(args):
        xb, e = args
        h = xb @ w_in[e] + b_in[e]
        glu = jnp.minimum(h[:, ::2], SWIGLU_LIMIT)
        lin = jnp.clip(h[:, 1::2], -SWIGLU_LIMIT, SWIGLU_LIMIT)
        act = glu * jax.nn.sigmoid(SWIGLU_ALPHA * glu) * (lin + 1.0)
        return act @ w_out[e] + b_out[e]

    y_buf = lax.map(expert_block,
                    (x_buf.reshape(n_blk, MOE_BLOCK, D), blk_expert)).reshape(P, D)
    y = y_buf[dest] * g_s[:, None].astype(y_buf.dtype)
    return jax.ops.segment_sum(y, tok_s, num_segments=N)


def setup_inputs(seed: int = 0) -> dict:
    key = jax.random.key(seed)
    ks = jax.random.split(key, 20)
    f32 = jnp.float32
    L = DEPTH
    x = jax.random.normal(ks[0], (BATCH, SEQ, D_MODEL), f32)
    positions = jnp.broadcast_to(jnp.arange(SEQ, dtype=jnp.int32), (BATCH, SEQ))
    norm1_g = 1.0 + 0.02 * jax.random.normal(ks[1], (L, D_MODEL), f32)
    w_in = jax.random.normal(ks[2], (L, D_MODEL, PROJ_WIDTH), f32) * D_MODEL ** -0.5
    lambda_q1 = 0.1 * jax.random.normal(ks[3], (L, DIFF_QK_DIM), f32)
    lambda_k1 = 0.1 * jax.random.normal(ks[4], (L, DIFF_QK_DIM), f32)
    lambda_q2 = 0.1 * jax.random.normal(ks[5], (L, DIFF_QK_DIM), f32)
    lambda_k2 = 0.1 * jax.random.normal(ks[6], (L, DIFF_QK_DIM), f32)
    diff_norm_g = 1.0 + 0.02 * jax.random.normal(ks[7], (L, DIFF_V_DIM), f32)
    ret_norm_g = 1.0 + 0.02 * jax.random.normal(ks[8], (L, N_RET_HEADS, RET_V_DIM), f32)
    w_o = jax.random.normal(ks[9], (L, MIX_WIDTH, D_MODEL), f32) * MIX_WIDTH ** -0.5
    norm2_g = 1.0 + 0.02 * jax.random.normal(ks[10], (L, D_MODEL), f32)
    w_router = jax.random.normal(ks[11], (L, D_MODEL, N_EXPERTS), f32) * D_MODEL ** -0.5
    b_router = 0.01 * jax.random.normal(ks[12], (L, N_EXPERTS), f32)
    w_moe_in = jax.random.normal(ks[13], (L, N_EXPERTS, D_MODEL, 2 * D_FF_EXPERT), f32) * D_MODEL ** -0.5
    b_moe_in = 0.01 * jax.random.normal(ks[14], (L, N_EXPERTS, 2 * D_FF_EXPERT), f32)
    w_moe_out = jax.random.normal(ks[15], (L, N_EXPERTS, D_FF_EXPERT, D_MODEL), f32) * D_FF_EXPERT ** -0.5
    b_moe_out = 0.01 * jax.random.normal(ks[16], (L, N_EXPERTS, D_MODEL), f32)
    norm_f_g = 1.0 + 0.02 * jax.random.normal(ks[17], (D_MODEL,), f32)
    return {'x': x, 'positions': positions, 'norm1_g': norm1_g, 'w_in': w_in,
            'lambda_q1': lambda_q1, 'lambda_k1': lambda_k1,
            'lambda_q2': lambda_q2, 'lambda_k2': lambda_k2,
            'diff_norm_g': diff_norm_g, 'ret_norm_g': ret_norm_g, 'w_o': w_o,
            'norm2_g': norm2_g, 'w_router': w_router, 'b_router': b_router,
            'w_moe_in': w_moe_in, 'b_moe_in': b_moe_in,
            'w_moe_out': w_moe_out, 'b_moe_out': b_moe_out, 'norm_f_g': norm_f_g}


def reference(x, positions, norm1_g, w_in, lambda_q1, lambda_k1, lambda_q2, lambda_k2,
              diff_norm_g, ret_norm_g, w_o, norm2_g, w_router, b_router,
              w_moe_in, b_moe_in, w_moe_out, b_moe_out, norm_f_g):
    B, S, D = x.shape
    log_decay = jnp.log(1.0 - 2.0 ** (-5.0 - jnp.arange(N_RET_HEADS, dtype=jnp.float32)))
    for l in range(DEPTH):
        h = rms_norm(x, norm1_g[l])
        proj = h @ w_in[l]
        dq, dk, dv, rq, rk, rv, rg = jnp.split(proj, PROJ_SPLITS, axis=-1)
        dq = rope(dq.reshape(B, S, N_DIFF_HEADS, 2, DIFF_QK_DIM), positions)
        dk = rope(dk.reshape(B, S, N_DIFF_HEADS, 2, DIFF_QK_DIM), positions)
        dv = dv.reshape(B, S, N_DIFF_HEADS, DIFF_V_DIM)
        lam_init = 0.8 - 0.6 * math.exp(-0.3 * l)
        lam = (jnp.exp(jnp.sum(lambda_q1[l].astype(jnp.float32) * lambda_k1[l].astype(jnp.float32)))
               - jnp.exp(jnp.sum(lambda_q2[l].astype(jnp.float32) * lambda_k2[l].astype(jnp.float32)))
               + lam_init)
        d_out = diff_attention(dq, dk, dv, lam)
        d_out = rms_norm(d_out, diff_norm_g[l]) * (1.0 - lam_init)
        rq = rope(rq.reshape(B, S, N_RET_HEADS, RET_QK_DIM), positions)
        rk = rope(rk.reshape(B, S, N_RET_HEADS, RET_QK_DIM), positions)
        rv = rv.reshape(B, S, N_RET_HEADS, RET_V_DIM)
        r_out = rms_norm(retention(rq, rk, rv, log_decay), ret_norm_g[l])
        r_out = r_out.reshape(B, S, RET_WIDTH) * jax.nn.silu(rg)
        mix = jnp.concatenate([d_out.reshape(B, S, DIFF_WIDTH), r_out], axis=-1)
        x = x + mix @ w_o[l]
        h2 = rms_norm(x, norm2_g[l]).reshape(B * S, D)
        y = moe(h2, w_router[l], b_router[l], w_moe_in[l], b_moe_in[l],
                w_moe_out[l], b_moe_out[l])
        x = x + y.reshape(B, S, D)
    return rms_norm(x, norm_f_g)
```

```python
import functools
import math

import numpy as np
import jax
import jax.numpy as jnp
from jax import lax
from jax.experimental import pallas as pl
from jax.experimental.pallas import tpu as pltpu

F32 = jnp.float32
BF16 = jnp.bfloat16
I32 = jnp.int32
U32 = jnp.uint32

N_DIFF_HEADS = 4
DIFF_QK_DIM = 64
DIFF_V_DIM = 128
DIFF_WIDTH = N_DIFF_HEADS * DIFF_V_DIM
N_RET_HEADS = 4
RET_QK_DIM = 64
RET_V_DIM = 128
RET_WIDTH = N_RET_HEADS * RET_V_DIM
ROPE_THETA = 10000.0
RMS_EPS = 1e-5
N_EXPERTS = 32
TOP_K = 4
SWIGLU_LIMIT = 7.0
SWIGLU_ALPHA = 1.702
COL_DQ = 0
COL_DK = 512
COL_DV = 1024
COL_RQ = 1536
COL_RK = 1792
COL_RV = 2048
COL_RG = 2560
PROJ_WIDTH = 3072
LOG_DECAY = tuple(math.log(1.0 - 2.0 ** (-5.0 - h)) for h in range(N_RET_HEADS))
LOG2E = 1.4426950408889634

LANES = 128
MXU_DIM = 256
VMEM_LIMIT = 56 * 1024 * 1024

ROPE_ROWS = 1024
PROJ_TM = 512
ATT_TQ = 512
ATT_TK = 512
RET_T = 512
RET_C = 256
OUT_TM = 512
PLAN_TC = 512
DISP_TM = 512
MOE_BM = 256
COMB_TM = 256
WPREP_TD = 512
DMA_UNROLL = 8

NEG_BIG = -1e30


def _cparams(*sem):
    return pltpu.CompilerParams(dimension_semantics=sem, vmem_limit_bytes=VMEM_LIMIT)


def _rope_table_kernel(pos_ref, invf_ref, cos_ref, sin_ref):
    invf = invf_ref[...]
    lane = lax.broadcasted_iota(I32, (LANES, LANES), 1)
    first_half = (lane & 32) == 0
    for r in range(ROPE_ROWS // LANES):
        prow = pos_ref[r:r + 1, :].astype(F32)
        pcol = jnp.broadcast_to(prow, (LANES, LANES)).T
        ang = pcol * invf
        s = jnp.sin(ang)
        cos_ref[r * LANES:(r + 1) * LANES, :] = jnp.cos(ang)
        sin_ref[r * LANES:(r + 1) * LANES, :] = jnp.where(first_half, -s, s)


def _rope_tables(positions):
    n = positions.size
    pos2d = positions.reshape(n // LANES, LANES)
    d = DIFF_QK_DIM
    inv_freq = 1.0 / (ROPE_THETA ** (jnp.arange(0, d, 2, dtype=F32) / d))
    invf = jnp.tile(inv_freq, LANES // (d // 2)).reshape(1, LANES)
    rows = ROPE_ROWS // LANES
    return pl.pallas_call(
        _rope_table_kernel,
        grid=(n // ROPE_ROWS,),
        in_specs=[pl.BlockSpec((rows, LANES), lambda i: (i, 0)),
                  pl.BlockSpec((1, LANES), lambda i: (0, 0))],
        out_specs=[pl.BlockSpec((ROPE_ROWS, LANES), lambda i: (i, 0)),
                   pl.BlockSpec((ROPE_ROWS, LANES), lambda i: (i, 0))],
        out_shape=[jax.ShapeDtypeStruct((n, LANES), F32)] * 2,
        compiler_params=_cparams("arbitrary"),
        name="rope_table",
    )(pos2d, invf)


def _inproj_kernel(x_ref, g_ref, w_ref, cos_ref, sin_ref, o_ref):
    x = x_ref[...]
    ms = jnp.mean(x * x, axis=-1, keepdims=True)
    h = (x * lax.rsqrt(ms + RMS_EPS) * g_ref[...]).astype(BF16)
    cos = cos_ref[...]
    sin = sin_ref[...]
    lane = lax.broadcasted_iota(I32, cos.shape, 1)
    first_half = (lane & 32) == 0

    def rope(t):
        rot = jnp.where(first_half, pltpu.roll(t, 96, 1), pltpu.roll(t, 32, 1))
        return t * cos + rot * sin

    q_scale = DIFF_QK_DIM ** -0.5 * LOG2E
    k_scale = RET_QK_DIM ** -0.5
    for c in range(PROJ_WIDTH // MXU_DIM):
        p = jnp.dot(h, w_ref[:, c * MXU_DIM:(c + 1) * MXU_DIM], preferred_element_type=F32)
        for half in range(MXU_DIM // LANES):
            col = c * MXU_DIM + half * LANES
            t = p[:, half * LANES:(half + 1) * LANES]
            if col < COL_DK:
                t = rope(t) * q_scale
            elif col < COL_DV:
                t = rope(t)
            elif col < COL_RQ:
                pass
            elif col < COL_RK:
                t = rope(t)
            elif col < COL_RV:
                t = rope(t) * k_scale
            elif col < COL_RG:
                pass
            else:
                t = t * jax.nn.sigmoid(t)
            o_ref[:, col:col + LANES] = t.astype(BF16)


def _in_proj(x2d, g1, w_in_b, cos, sin):
    n, d = x2d.shape
    tm = PROJ_TM
    return pl.pallas_call(
        _inproj_kernel,
        grid=(n // tm,),
        in_specs=[pl.BlockSpec((tm, d), lambda i: (i, 0)),
                  pl.BlockSpec((1, d), lambda i: (0, 0)),
                  pl.BlockSpec((d, PROJ_WIDTH), lambda i: (0, 0)),
                  pl.BlockSpec((tm, LANES), lambda i: (i, 0)),
                  pl.BlockSpec((tm, LANES), lambda i: (i, 0))],
        out_specs=pl.BlockSpec((tm, PROJ_WIDTH), lambda i: (i, 0)),
        out_shape=jax.ShapeDtypeStruct((n, PROJ_WIDTH), BF16),
        compiler_params=_cparams("arbitrary"),
        name="in_proj",
    )(x2d, g1.reshape(1, d), w_in_b, cos, sin)


def _diff_attn_kernel(lq1_ref, lk1_ref, lq2_ref, lk2_ref, g_ref, q_ref, k_ref, v_ref, o_ref,
                      qq_sc, m_sc, l_sc, acc_sc, *, lam_init):
    tq, tk = ATT_TQ, ATT_TK
    i = pl.program_id(2)
    q = q_ref[...]
    lane = lax.broadcasted_iota(I32, q.shape, 1)
    zero = jnp.zeros_like(q)
    qq_sc[:tq, :] = jnp.where(lane < DIFF_QK_DIM, q, zero)
    qq_sc[tq:, :] = jnp.where(lane >= DIFF_QK_DIM, q, zero)
    m_sc[...] = jnp.full(m_sc.shape, NEG_BIG, F32)
    l_sc[...] = jnp.zeros(l_sc.shape, F32)
    acc_sc[...] = jnp.zeros(acc_sc.shape, F32)

    def step(j, masked):
        start = pl.multiple_of(j * tk, tk)
        k = k_ref[pl.ds(start, tk), :]
        v = v_ref[pl.ds(start, tk), :]
        s = lax.dot_general(qq_sc[...], k, (((1,), (1,)), ((), ())), preferred_element_type=F32)
        if masked:
            row = lax.broadcasted_iota(I32, s.shape, 0) & (tq - 1)
            col = lax.broadcasted_iota(I32, s.shape, 1)
            s = jnp.where(col <= row, s, NEG_BIG)
        m_prev = m_sc[...]
        m_next = jnp.maximum(m_prev, jnp.max(s, axis=1, keepdims=True))
        alpha = jnp.exp2(m_prev - m_next)
        p = jnp.exp2(s - jnp.concatenate([m_next] * (tk // LANES), axis=1))
        l_sc[...] = alpha * l_sc[...] + jnp.sum(p, axis=1, keepdims=True)
        acc_sc[...] = alpha * acc_sc[...] + jnp.dot(p.astype(BF16), v, preferred_element_type=F32)
        m_sc[...] = m_next

    def body(j, carry):
        step(j, False)
        return carry

    lax.fori_loop(0, i, body, 0)
    step(i, True)

    lam = (jnp.exp(jnp.sum(lq1_ref[...] * lk1_ref[...], axis=-1, keepdims=True))
           - jnp.exp(jnp.sum(lq2_ref[...] * lk2_ref[...], axis=-1, keepdims=True))
           + lam_init)
    o = acc_sc[...] / l_sc[...]
    d = o[:tq, :] - lam * o[tq:, :]
    ms = jnp.mean(d * d, axis=-1, keepdims=True)
    out = d * lax.rsqrt(ms + RMS_EPS) * g_ref[...] * (1.0 - lam_init)
    o_ref[...] = out.astype(BF16)


def _diff_attention(proj, lq1, lk1, lq2, lk2, g, batch, seq, lam_init):
    n = batch * seq
    tq = ATT_TQ
    assert ATT_TQ == ATT_TK and seq % tq == 0 and DIFF_V_DIM == LANES
    nq = seq // tq
    qcol = COL_DQ // LANES
    kcol = COL_DK // LANES
    vcol = COL_DV // LANES
    vec = lambda b, h, i: (0, 0)
    return pl.pallas_call(
        functools.partial(_diff_attn_kernel, lam_init=lam_init),
        grid=(batch, N_DIFF_HEADS, nq),
        in_specs=[pl.BlockSpec((1, DIFF_QK_DIM), vec)] * 4 + [
            pl.BlockSpec((1, DIFF_V_DIM), vec),
            pl.BlockSpec((tq, LANES), lambda b, h, i: (b * nq + i, qcol + h)),
            pl.BlockSpec((seq, LANES), lambda b, h, i: (b, kcol + h)),
            pl.BlockSpec((seq, LANES), lambda b, h, i: (b, vcol + h))],
        out_specs=pl.BlockSpec((tq, LANES), lambda b, h, i: (b * nq + i, h)),
        out_shape=jax.ShapeDtypeStruct((n, DIFF_WIDTH), BF16),
        scratch_shapes=[pltpu.VMEM((2 * tq, LANES), BF16),
                        pltpu.VMEM((2 * tq, LANES), F32),
                        pltpu.VMEM((2 * tq, LANES), F32),
                        pltpu.VMEM((2 * tq, DIFF_V_DIM), F32)],
        compiler_params=_cparams("arbitrary", "arbitrary", "arbitrary"),
        name="diff_attn",
    )(lq1.reshape(1, -1), lk1.reshape(1, -1), lq2.reshape(1, -1), lk2.reshape(1, -1),
      g.reshape(1, -1), proj, proj, proj)


def _retention_kernel(q_ref, k_ref, v_ref, gate_ref, g_ref, o_ref, state_sc):
    c_len = RET_C

    @pl.when(pl.program_id(1) == 0)
    def _():
        state_sc[...] = jnp.zeros(state_sc.shape, F32)

    ii = lax.broadcasted_iota(I32, (c_len, c_len), 0)
    jj = lax.broadcasted_iota(I32, (c_len, c_len), 1)
    rel = (ii - jj).astype(F32)
    lane = lax.broadcasted_iota(I32, (c_len, LANES), 1)
    pos = lax.broadcasted_iota(I32, (c_len, LANES), 0).astype(F32)
    srow = lax.broadcasted_iota(I32, (LANES, LANES), 0)
    for pair in range(N_RET_HEADS // 2):
        ld = (LOG_DECAY[2 * pair], LOG_DECAY[2 * pair + 1])
        ld_lane = jnp.where(lane < RET_QK_DIM, ld[0], ld[1])
        q_decay = jnp.exp(ld_lane * (pos + 1.0))
        k_decay = jnp.exp(ld_lane * (c_len - 1.0 - pos))
        chunk_decay = jnp.where(srow < RET_QK_DIM, math.exp(ld[0] * c_len), math.exp(ld[1] * c_len))
        intra = [jnp.where(rel >= 0, jnp.exp(l * jnp.maximum(rel, 0.0)), 0.0) for l in ld]
        in_head = (lane < RET_QK_DIM, lane >= RET_QK_DIM)
        for c in range(RET_T // c_len):
            rows = slice(c * c_len, (c + 1) * c_len)
            qb = q_ref[rows, pair * LANES:(pair + 1) * LANES]
            kb = k_ref[rows, pair * LANES:(pair + 1) * LANES]
            q = qb.astype(F32)
            state = state_sc[pair]
            state_b = state.astype(BF16)
            kd_t = (kb.astype(F32) * k_decay).T.astype(BF16)
            new_kv = []
            for hh in range(2):
                h = 2 * pair + hh
                qm = jnp.where(in_head[hh], q, 0.0)
                s = lax.dot_general(qm.astype(BF16), kb, (((1,), (1,)), ((), ())),
                                    preferred_element_type=F32) * intra[hh]
                v = v_ref[rows, h * RET_V_DIM:(h + 1) * RET_V_DIM]
                y = (jnp.dot(s.astype(BF16), v, preferred_element_type=F32)
                     + jnp.dot((qm * q_decay).astype(BF16), state_b, preferred_element_type=F32))
                new_kv.append(jnp.dot(kd_t, v, preferred_element_type=F32))
                ms = jnp.mean(y * y, axis=-1, keepdims=True)
                yn = y * lax.rsqrt(ms + RMS_EPS) * g_ref[h:h + 1, :]
                gate = gate_ref[rows, h * RET_V_DIM:(h + 1) * RET_V_DIM].astype(F32)
                o_ref[rows, h * RET_V_DIM:(h + 1) * RET_V_DIM] = (yn * gate).astype(BF16)
            state_sc[pair] = chunk_decay * state + jnp.where(srow < RET_QK_DIM, new_kv[0], new_kv[1])


def _retention(proj, g, batch, seq):
    n = batch * seq
    t = RET_T
    nt = seq // t
    qk_w = N_RET_HEADS * RET_QK_DIM
    return pl.pallas_call(
        _retention_kernel,
        grid=(batch, nt),
        in_specs=[pl.BlockSpec((t, qk_w), lambda b, i: (b * nt + i, COL_RQ // qk_w)),
                  pl.BlockSpec((t, qk_w), lambda b, i: (b * nt + i, COL_RK // qk_w)),
                  pl.BlockSpec((t, RET_WIDTH), lambda b, i: (b * nt + i, COL_RV // RET_WIDTH)),
                  pl.BlockSpec((t, RET_WIDTH), lambda b, i: (b * nt + i, COL_RG // RET_WIDTH)),
                  pl.BlockSpec((N_RET_HEADS, RET_V_DIM), lambda b, i: (0, 0))],
        out_specs=pl.BlockSpec((t, RET_WIDTH), lambda b, i: (b * nt + i, 0)),
        out_shape=jax.ShapeDtypeStruct((n, RET_WIDTH), BF16),
        scratch_shapes=[pltpu.VMEM((N_RET_HEADS // 2, LANES, RET_V_DIM), F32)],
        compiler_params=_cparams("arbitrary", "arbitrary"),
        name="retention",
    )(proj, proj, proj, proj, g)


def _outproj_kernel(d_ref, r_ref, wo_ref, x_ref, g2_ref, wr_ref, br_ref,
                    x1_ref, h2p_ref, idx_ref, gate_ref):
    acc = (jnp.dot(d_ref[...], wo_ref[:DIFF_WIDTH, :], preferred_element_type=F32)
           + jnp.dot(r_ref[...], wo_ref[DIFF_WIDTH:, :], preferred_element_type=F32))
    x1 = x_ref[...] + acc
    x1_ref[...] = x1
    ms = jnp.mean(x1 * x1, axis=-1, keepdims=True)
    h2 = x1 * lax.rsqrt(ms + RMS_EPS) * g2_ref[...]
    half = h2.shape[1] // 2
    h2p_ref[...] = pltpu.pack_elementwise([h2[:, :half], h2[:, half:]], packed_dtype=BF16)

    logits = lax.dot_general(wr_ref[...], h2, (((1,), (1,)), ((), ())),
                             precision=lax.Precision.HIGHEST,
                             preferred_element_type=F32) + br_ref[...]
    e_iota = lax.broadcasted_iota(I32, logits.shape, 0)
    vals = []
    for r in range(TOP_K):
        m = jnp.max(logits, axis=0, keepdims=True)
        ix = jnp.min(jnp.where(logits == m, e_iota, N_EXPERTS), axis=0, keepdims=True)
        vals.append(m)
        idx_ref[r:r + 1, :] = ix
        logits = jnp.where(e_iota == ix, -jnp.inf, logits)
    ex = [jnp.exp(v - vals[0]) for v in vals]
    den = ex[0] + ex[1] + ex[2] + ex[3]
    for r in range(TOP_K):
        gate_ref[r:r + 1, :] = ex[r] / den


def _out_proj(d_out, r_out, w_o_b, x2d, g2, w_router_t, b_router):
    n, d = x2d.shape
    tm = OUT_TM
    const = lambda i: (0, 0)
    return pl.pallas_call(
        _outproj_kernel,
        grid=(n // tm,),
        in_specs=[pl.BlockSpec((tm, DIFF_WIDTH), lambda i: (i, 0)),
                  pl.BlockSpec((tm, RET_WIDTH), lambda i: (i, 0)),
                  pl.BlockSpec((DIFF_WIDTH + RET_WIDTH, d), const),
                  pl.BlockSpec((tm, d), lambda i: (i, 0)),
                  pl.BlockSpec((1, d), const),
                  pl.BlockSpec((N_EXPERTS, d), const),
                  pl.BlockSpec((N_EXPERTS, 1), const)],
        out_specs=[pl.BlockSpec((tm, d), lambda i: (i, 0)),
                   pl.BlockSpec((tm, d // 2), lambda i: (i, 0)),
                   pl.BlockSpec((TOP_K, tm), lambda i: (0, i)),
                   pl.BlockSpec((TOP_K, tm), lambda i: (0, i))],
        out_shape=[jax.ShapeDtypeStruct((n, d), F32),
                   jax.ShapeDtypeStruct((n, d // 2), U32),
                   jax.ShapeDtypeStruct((TOP_K, n), I32),
                   jax.ShapeDtypeStruct((TOP_K, n), F32)],
        compiler_params=_cparams("arbitrary"),
        name="out_proj",
    )(d_out, r_out, w_o_b, x2d, g2.reshape(1, d), w_router_t, b_router.reshape(N_EXPERTS, 1))


def _plan_kernel(idx_ref, dest_ref, blk_ref, ends_ref, cnt_sc, base_sc, tri_sc, *, nb_pad):
    ph = pl.program_id(0)
    c = pl.program_id(1)
    tc = PLAN_TC
    e_iota = lax.broadcasted_iota(I32, (N_EXPERTS, tc), 0)

    @pl.when((ph == 0) & (c == 0))
    def _():
        cnt_sc[...] = jnp.zeros(cnt_sc.shape, F32)
        s = lax.broadcasted_iota(I32, (tc, tc), 0)
        t = lax.broadcasted_iota(I32, (tc, tc), 1)
        tri_sc[...] = jnp.where(s < t, 1.0, 0.0).astype(BF16)

    @pl.when(ph == 0)
    def _():
        tot = jnp.zeros((N_EXPERTS, 1), F32)
        for k in range(TOP_K):
            oh = idx_ref[k:k + 1, :] == e_iota
            tot = tot + jnp.sum(jnp.where(oh, 1.0, 0.0), axis=1, keepdims=True)
        cnt_sc[...] = cnt_sc[...] + tot

    @pl.when((ph == 1) & (c == 0))
    def _():
        cnt = cnt_sc[...]
        nblk = jnp.floor((cnt + (MOE_BM - 1.0)) * (1.0 / MOE_BM))
        ei = lax.broadcasted_iota(I32, (N_EXPERTS, LANES), 0)
        li = lax.broadcasted_iota(I32, (N_EXPERTS, LANES), 1)
        nblk_row = jnp.sum(jnp.where(ei == li, nblk, 0.0), axis=0, keepdims=True)
        start = jnp.sum(jnp.where(li < ei, nblk_row, 0.0), axis=1, keepdims=True)
        base_sc[...] = start * MOE_BM
        end = start + nblk
        bi = lax.broadcasted_iota(I32, (N_EXPERTS, nb_pad), 1).astype(F32)
        be = jnp.sum(jnp.where(end <= bi, 1.0, 0.0), axis=0, keepdims=True)
        blk_ref[...] = jnp.minimum(be, N_EXPERTS - 1.0).astype(I32)
        ends_ref[...] = jnp.sum(jnp.where(ei == li, end, 0.0), axis=0, keepdims=True).astype(I32)

    @pl.when(ph == 1)
    def _():
        base = base_sc[...]
        for k in range(TOP_K):
            oh = idx_ref[k:k + 1, :] == e_iota
            ohb = jnp.where(oh, 1.0, 0.0).astype(BF16)
            before = jnp.dot(ohb, tri_sc[...], preferred_element_type=F32)
            rank = jnp.sum(jnp.where(oh, before + base, 0.0), axis=0, keepdims=True)
            dest_ref[k:k + 1, :] = rank.astype(I32)
            base = base + jnp.sum(jnp.where(oh, 1.0, 0.0), axis=1, keepdims=True)
        base_sc[...] = base


def _plan(top_idx, nb_pad):
    n = top_idx.shape[1]
    tc = PLAN_TC
    return pl.pallas_call(
        functools.partial(_plan_kernel, nb_pad=nb_pad),
        grid=(2, n // tc),
        in_specs=[pl.BlockSpec((TOP_K, tc), lambda ph, c: (0, c))],
        out_specs=[pl.BlockSpec((TOP_K, tc), lambda ph, c: (0, c * ph)),
                   pl.BlockSpec((1, nb_pad), lambda ph, c: (0, 0)),
                   pl.BlockSpec((1, LANES), lambda ph, c: (0, 0))],
        out_shape=[jax.ShapeDtypeStruct((TOP_K, n), I32),
                   jax.ShapeDtypeStruct((1, nb_pad), I32),
                   jax.ShapeDtypeStruct((1, LANES), I32)],
        scratch_shapes=[pltpu.VMEM((N_EXPERTS, 1), F32),
                        pltpu.VMEM((N_EXPERTS, 1), F32),
                        pltpu.VMEM((tc, tc), BF16)],
        compiler_params=_cparams("arbitrary", "arbitrary"),
        name="plan",
    )(top_idx)


def _dispatch_kernel(ends_ref, dest_ref, h_ref, xbuf_ref, zbuf, sem, zsem, *, nb):
    tm = DISP_TM
    bm = MOE_BM

    @pl.when(pl.program_id(0) == 0)
    def _():
        zbuf[...] = jnp.zeros(zbuf.shape, zbuf.dtype)

        def zero_block(blk):
            return pltpu.make_async_copy(zbuf, xbuf_ref.at[pl.ds(pl.multiple_of(blk * bm, bm), bm)], zsem)

        def per_block(fn):
            for e in range(N_EXPERTS):
                end = ends_ref[e]
                first = ends_ref[e - 1] if e else 0

                @pl.when(end > first)
                def _():
                    fn(zero_block(end - 1))

            def tail(blk, carry):
                fn(zero_block(blk))
                return carry

            lax.fori_loop(ends_ref[N_EXPERTS - 1], nb, tail, 0)

        per_block(lambda cp: cp.start())
        per_block(lambda cp: cp.wait())

    def issue(i, carry):
        for u in range(DMA_UNROLL):
            t = i * DMA_UNROLL + u
            for k in range(TOP_K):
                pltpu.make_async_copy(h_ref.at[pl.ds(t, 1)],
                                      xbuf_ref.at[pl.ds(dest_ref[k, t], 1)], sem).start()
        return carry

    lax.fori_loop(0, tm // DMA_UNROLL, issue, 0)
    for k in range(TOP_K):
        pltpu.make_async_copy(h_ref, xbuf_ref.at[pl.ds(0, tm)], sem).wait()


def _dispatch(ends, dest, h2p, p_rows):
    n, w = h2p.shape
    tm = DISP_TM
    grid_spec = pltpu.PrefetchScalarGridSpec(
        num_scalar_prefetch=1,
        grid=(n // tm,),
        in_specs=[pl.BlockSpec((TOP_K, tm), lambda i, ends: (0, i), memory_space=pltpu.SMEM),
                  pl.BlockSpec((tm, w), lambda i, ends: (i, 0))],
        out_specs=pl.BlockSpec(memory_space=pl.ANY),
        scratch_shapes=[pltpu.VMEM((MOE_BM, w), h2p.dtype),
                        pltpu.SemaphoreType.DMA(()),
                        pltpu.SemaphoreType.DMA(())],
    )
    return pl.pallas_call(
        functools.partial(_dispatch_kernel, nb=p_rows // MOE_BM),
        grid_spec=grid_spec,
        out_shape=jax.ShapeDtypeStruct((p_rows, w), h2p.dtype),
        compiler_params=pltpu.CompilerParams(dimension_semantics=("arbitrary",),
                                             vmem_limit_bytes=VMEM_LIMIT,
                                             has_side_effects=True),
        name="dispatch",
    )(ends, dest, h2p)


def _pair_perm():
    a = lax.broadcasted_iota(I32, (MXU_DIM, MXU_DIM), 0)
    b = lax.broadcasted_iota(I32, (MXU_DIM, MXU_DIM), 1)
    src = jnp.where(b < LANES, 2 * b, 2 * (b - LANES) + 1)
    return jnp.where(a == src, 1.0, 0.0).astype(BF16)


def _wprep_kernel(w_ref, o_ref):
    perm = _pair_perm()
    for c in range(w_ref.shape[2] // MXU_DIM):
        cols = slice(c * MXU_DIM, (c + 1) * MXU_DIM)
        blk = w_ref[0, :, cols].astype(BF16)
        o_ref[0, :, cols] = jnp.dot(blk, perm, preferred_element_type=F32).astype(BF16)


def _prep_expert_in_weights(w):
    e, d, f2 = w.shape
    td = WPREP_TD
    return pl.pallas_call(
        _wprep_kernel,
        grid=(e, d // td),
        in_specs=[pl.BlockSpec((1, td, f2), lambda i, j: (i, j, 0))],
        out_specs=pl.BlockSpec((1, td, f2), lambda i, j: (i, j, 0)),
        out_shape=jax.ShapeDtypeStruct((e, d, f2), BF16),
        compiler_params=_cparams("arbitrary", "arbitrary"),
        name="expert_w_prep",
    )(w)


def _pair_group(b):
    e, f2 = b.shape
    return b.reshape(e, f2 // MXU_DIM, LANES, 2).transpose(0, 1, 3, 2).reshape(e, 1, f2)


def _expert_kernel(blk_ref, ends_ref, x_ref, w1_ref, b1_ref, w2_ref, b2_ref, y_ref):
    b = pl.program_id(0)
    n_used = ends_ref[N_EXPERTS - 1]

    @pl.when(b >= n_used)
    def _():
        y_ref[...] = jnp.zeros(y_ref.shape, y_ref.dtype)

    @pl.when(b < n_used)
    def _():
        w = x_ref[...]
        lo = pltpu.unpack_elementwise(w, index=0, packed_dtype=BF16, unpacked_dtype=F32)
        hi = pltpu.unpack_elementwise(w, index=1, packed_dtype=BF16, unpacked_dtype=F32)
        x = jnp.concatenate([lo.astype(BF16), hi.astype(BF16)], axis=1)
        acc = jnp.zeros((x.shape[0], w2_ref.shape[2]), F32)
        grp = 2 * MXU_DIM
        for c in range(w1_ref.shape[2] // grp):
            cols = slice(c * grp, (c + 1) * grp)
            h = jnp.dot(x, w1_ref[0, :, cols], preferred_element_type=F32) + b1_ref[0, :, cols]
            glu = jnp.concatenate([h[:, 0:LANES], h[:, 2 * LANES:3 * LANES]], axis=1)
            lin = jnp.concatenate([h[:, LANES:2 * LANES], h[:, 3 * LANES:4 * LANES]], axis=1)
            glu = jnp.minimum(glu, SWIGLU_LIMIT)
            lin = jnp.clip(lin, -SWIGLU_LIMIT, SWIGLU_LIMIT)
            act = glu * jax.nn.sigmoid(SWIGLU_ALPHA * glu) * (lin + 1.0)
            acc = acc + jnp.dot(act.astype(BF16), w2_ref[0, c * MXU_DIM:(c + 1) * MXU_DIM, :],
                                preferred_element_type=F32)
        y_ref[...] = acc + b2_ref[0]


def _experts(blk_expert, ends, x_buf, w1, b1, w2, b2):
    p_rows, half = x_buf.shape
    e, d, f2 = w1.shape
    f = f2 // 2
    bm = MOE_BM
    nb = p_rows // bm

    def row_blk(b, ends):
        return jnp.minimum(b, ends[N_EXPERTS - 1] - 1)

    grid_spec = pltpu.PrefetchScalarGridSpec(
        num_scalar_prefetch=2,
        grid=(nb,),
        in_specs=[pl.BlockSpec((bm, half), lambda b, blk, ends: (row_blk(b, ends), 0)),
                  pl.BlockSpec((1, d, f2), lambda b, blk, ends: (blk[row_blk(b, ends)], 0, 0)),
                  pl.BlockSpec((1, 1, f2), lambda b, blk, ends: (blk[row_blk(b, ends)], 0, 0)),
                  pl.BlockSpec((1, f, d), lambda b, blk, ends: (blk[row_blk(b, ends)], 0, 0)),
                  pl.BlockSpec((1, 1, d), lambda b, blk, ends: (blk[row_blk(b, ends)], 0, 0))],
        out_specs=pl.BlockSpec((bm, d), lambda b, blk, ends: (b, 0)),
    )
    return pl.pallas_call(
        _expert_kernel,
        grid_spec=grid_spec,
        out_shape=jax.ShapeDtypeStruct((p_rows, d), F32),
        compiler_params=_cparams("arbitrary"),
        name="experts",
    )(blk_expert, ends, x_buf, w1, b1, w2, b2)


def _combine_kernel(dest_ref, gate_ref, x1_ref, gf_ref, ybuf_ref, o_ref, gbuf, sem):
    tm = COMB_TM

    def issue(i, carry):
        for u in range(DMA_UNROLL):
            t = i * DMA_UNROLL + u
            for k in range(TOP_K):
                pltpu.make_async_copy(ybuf_ref.at[pl.ds(dest_ref[k, t], 1)],
                                      gbuf.at[k, pl.ds(t, 1)], sem).start()
        return carry

    lax.fori_loop(0, tm // DMA_UNROLL, issue, 0)
    for k in range(TOP_K):
        pltpu.make_async_copy(ybuf_ref.at[pl.ds(0, tm)], gbuf.at[k], sem).wait()

    gates = gate_ref[...]
    pad = jnp.zeros((LANES - TOP_K, LANES), F32)
    cols = []
    for c in range(tm // LANES):
        blk = jnp.concatenate([gates[:, c * LANES:(c + 1) * LANES], pad], axis=0)
        cols.append(blk.T)
    gcol = jnp.concatenate(cols, axis=0)
    acc = x1_ref[...]
    for k in range(TOP_K):
        acc = acc + gbuf[k] * gcol[:, k:k + 1]
    ms = jnp.mean(acc * acc, axis=-1, keepdims=True)
    o_ref[...] = acc * lax.rsqrt(ms + RMS_EPS) * gf_ref[...]


def _combine(dest, gates, x1, gf, y_buf):
    n, d = x1.shape
    tm = COMB_TM
    return pl.pallas_call(
        _combine_kernel,
        grid=(n // tm,),
        in_specs=[pl.BlockSpec((TOP_K, tm), lambda i: (0, i), memory_space=pltpu.SMEM),
                  pl.BlockSpec((TOP_K, tm), lambda i: (0, i)),
                  pl.BlockSpec((tm, d), lambda i: (i, 0)),
                  pl.BlockSpec((1, d), lambda i: (0, 0)),
                  pl.BlockSpec(memory_space=pl.ANY)],
        out_specs=pl.BlockSpec((tm, d), lambda i: (i, 0)),
        out_shape=jax.ShapeDtypeStruct((n, d), F32),
        scratch_shapes=[pltpu.VMEM((TOP_K, tm, d), F32),
                        pltpu.SemaphoreType.DMA(())],
        compiler_params=_cparams("arbitrary"),
        name="combine",
    )(dest, gates, x1, gf.reshape(1, d), y_buf)


def kernel(x, positions, norm1_g, w_in, lambda_q1, lambda_k1, lambda_q2, lambda_k2, diff_norm_g, ret_norm_g, w_o, norm2_g, w_router, b_router, w_moe_in, b_moe_in, w_moe_out, b_moe_out, norm_f_g):
    batch, seq, d = x.shape
    n = batch * seq
    assert norm1_g.shape[0] == 1, "single-layer block"
    l = 0
    p_rows = n * TOP_K + N_EXPERTS * MOE_BM
    nb = p_rows // MOE_BM
    nb_pad = -(-nb // LANES) * LANES

    cos, sin = _rope_tables(positions)
    x2d = x.reshape(n, d)
    lam_init = 0.8 - 0.6 * math.exp(-0.3 * l)
    proj = _in_proj(x2d, norm1_g[l], w_in[l].astype(BF16), cos, sin)
    d_out = _diff_attention(proj, lambda_q1[l], lambda_k1[l], lambda_q2[l], lambda_k2[l],
                            diff_norm_g[l], batch, seq, lam_init)
    r_out = _retention(proj, ret_norm_g[l], batch, seq)
    x1, h2p, top_idx, gates = _out_proj(d_out, r_out, w_o[l].astype(BF16), x2d, norm2_g[l],
                                        w_router[l].T, b_router[l])
    dest, blk_expert, ends = _plan(top_idx, nb_pad)
    ends = ends.reshape(-1)
    x_buf = _dispatch(ends, dest, h2p, p_rows)
    w1 = _prep_expert_in_weights(w_moe_in[l])
    b1 = _pair_group(b_moe_in[l])
    w2 = w_moe_out[l].astype(BF16)
    b2 = b_moe_out[l].reshape(N_EXPERTS, 1, -1)
    y_buf = _experts(blk_expert.reshape(-1), ends, x_buf, w1, b1, w2, b2)
    out = _combine(dest, gates, x1, norm_f_g, y_buf)
    return out.reshape(batch, seq, d)
```

```python
import functools
import math

import numpy as np
import jax
import jax.numpy as jnp
from jax import lax
from jax.experimental import pallas as pl
from jax.experimental.pallas import tpu as pltpu

F32 = jnp.float32
BF16 = jnp.bfloat16
I32 = jnp.int32
U32 = jnp.uint32

N_DIFF_HEADS = 4
DIFF_QK_DIM = 64
DIFF_V_DIM = 128
DIFF_WIDTH = N_DIFF_HEADS * DIFF_V_DIM
N_RET_HEADS = 4
RET_QK_DIM = 64
RET_V_DIM = 128
RET_WIDTH = N_RET_HEADS * RET_V_DIM
ROPE_THETA = 10000.0
RMS_EPS = 1e-5
N_EXPERTS = 32
TOP_K = 4
SWIGLU_LIMIT = 7.0
SWIGLU_ALPHA = 1.702
TOK_PER_ROW = 128 // TOP_K
COL_DQ = 0
COL_DK = 512
COL_DV = 1024
COL_RQ = 1536
COL_RK = 1792
COL_RV = 2048
COL_RG = 2560
PROJ_WIDTH = 3072
LOG_DECAY = tuple(math.log(1.0 - 2.0 ** (-5.0 - h)) for h in range(N_RET_HEADS))
LOG2E = 1.4426950408889634

LANES = 128
SUBLANES = 8
MXU_DIM = 256
VMEM_LIMIT = 56 * 1024 * 1024

ROPE_ROWS = 1024
PROJ_TM = 512
ATT_TQ = 1024
ATT_TK = 1024
RET_T = 512
RET_C = 256
OUT_TM = 512
PLAN_TC = 512
DISP_TM = 1024
MOE_BM = 512
COMB_TM = 512
WPREP_TD = 512

NEG_BIG = -1e30


def _cparams(*sem):
    return pltpu.CompilerParams(dimension_semantics=sem, vmem_limit_bytes=VMEM_LIMIT)


def _rope_table_kernel(pos_ref, invf_ref, cos_ref, sin_ref):
    invf = invf_ref[...]
    lane = lax.broadcasted_iota(I32, (LANES, LANES), 1)
    first_half = (lane & 32) == 0
    for r in range(ROPE_ROWS // LANES):
        prow = pos_ref[r:r + 1, :].astype(F32)
        pcol = jnp.broadcast_to(prow, (LANES, LANES)).T
        ang = pcol * invf
        s = jnp.sin(ang)
        cos_ref[r * LANES:(r + 1) * LANES, :] = jnp.cos(ang)
        sin_ref[r * LANES:(r + 1) * LANES, :] = jnp.where(first_half, -s, s)


def _rope_tables(positions):
    n = positions.size
    pos2d = positions.reshape(n // LANES, LANES)
    d = DIFF_QK_DIM
    inv_freq = 1.0 / (ROPE_THETA ** (jnp.arange(0, d, 2, dtype=F32) / d))
    invf = jnp.tile(inv_freq, LANES // (d // 2)).reshape(1, LANES)
    rows = ROPE_ROWS // LANES
    return pl.pallas_call(
        _rope_table_kernel,
        grid=(n // ROPE_ROWS,),
        in_specs=[pl.BlockSpec((rows, LANES), lambda i: (i, 0)),
                  pl.BlockSpec((1, LANES), lambda i: (0, 0))],
        out_specs=[pl.BlockSpec((ROPE_ROWS, LANES), lambda i: (i, 0)),
                   pl.BlockSpec((ROPE_ROWS, LANES), lambda i: (i, 0))],
        out_shape=[jax.ShapeDtypeStruct((n, LANES), F32)] * 2,
        compiler_params=_cparams("arbitrary"),
        name="rope_table",
    )(pos2d, invf)


def _inproj_kernel(x_ref, g_ref, w_ref, cos_ref, sin_ref, o_ref):
    x = x_ref[...]
    ms = jnp.mean(x * x, axis=-1, keepdims=True)
    h = (x * lax.rsqrt(ms + RMS_EPS) * g_ref[...]).astype(BF16)
    cos = cos_ref[...]
    sin = sin_ref[...]
    lane = lax.broadcasted_iota(I32, cos.shape, 1)
    first_half = (lane & 32) == 0

    def rope(t):
        rot = jnp.where(first_half, pltpu.roll(t, 96, 1), pltpu.roll(t, 32, 1))
        return t * cos + rot * sin

    q_scale = DIFF_QK_DIM ** -0.5 * LOG2E
    k_scale = RET_QK_DIM ** -0.5
    for c in range(PROJ_WIDTH // MXU_DIM):
        p = jnp.dot(h, w_ref[:, c * MXU_DIM:(c + 1) * MXU_DIM], preferred_element_type=F32)
        for half in range(MXU_DIM // LANES):
            col = c * MXU_DIM + half * LANES
            t = p[:, half * LANES:(half + 1) * LANES]
            if col < COL_DK:
                t = rope(t) * q_scale
            elif col < COL_DV:
                t = rope(t)
            elif col < COL_RQ:
                pass
            elif col < COL_RK:
                t = rope(t)
            elif col < COL_RV:
                t = rope(t) * k_scale
            elif col < COL_RG:
                pass
            else:
                t = t * jax.nn.sigmoid(t)
            o_ref[:, col:col + LANES] = t.astype(BF16)


def _in_proj(x2d, g1, w_in_b, cos, sin):
    n, d = x2d.shape
    tm = PROJ_TM
    return pl.pallas_call(
        _inproj_kernel,
        grid=(n // tm,),
        in_specs=[pl.BlockSpec((tm, d), lambda i: (i, 0)),
                  pl.BlockSpec((1, d), lambda i: (0, 0)),
                  pl.BlockSpec((d, PROJ_WIDTH), lambda i: (0, 0)),
                  pl.BlockSpec((tm, LANES), lambda i: (i, 0)),
                  pl.BlockSpec((tm, LANES), lambda i: (i, 0))],
        out_specs=pl.BlockSpec((tm, PROJ_WIDTH), lambda i: (i, 0)),
        out_shape=jax.ShapeDtypeStruct((n, PROJ_WIDTH), BF16),
        compiler_params=_cparams("arbitrary"),
        name="in_proj",
    )(x2d, g1.reshape(1, d), w_in_b, cos, sin)


def _diff_attn_kernel(lq1_ref, lk1_ref, lq2_ref, lk2_ref, g_ref, q_ref, k_ref, v_ref, o_ref,
                      qq_sc, m_sc, l_sc, acc_sc, *, lam_init):
    tq, tk = ATT_TQ, ATT_TK
    i = pl.program_id(2)
    q = q_ref[...]
    lane = lax.broadcasted_iota(I32, q.shape, 1)
    zero = jnp.zeros_like(q)
    qq_sc[:tq, :] = jnp.where(lane < DIFF_QK_DIM, q, zero)
    qq_sc[tq:, :] = jnp.where(lane >= DIFF_QK_DIM, q, zero)
    m_sc[...] = jnp.full(m_sc.shape, NEG_BIG, F32)
    l_sc[...] = jnp.zeros(l_sc.shape, F32)
    acc_sc[...] = jnp.zeros(acc_sc.shape, F32)

    def step(start, width, masked):
        k = k_ref[pl.ds(start, width), :]
        v = v_ref[pl.ds(start, width), :]
        s = lax.dot_general(qq_sc[...], k, (((1,), (1,)), ((), ())), preferred_element_type=F32)
        if masked:
            row = lax.broadcasted_iota(I32, s.shape, 0) & (tq - 1)
            col = lax.broadcasted_iota(I32, s.shape, 1)
            s = jnp.where(col <= row, s, NEG_BIG)
        m_prev = m_sc[...]
        m_next = jnp.maximum(m_prev, jnp.max(s, axis=1, keepdims=True))
        alpha = jnp.exp2(m_prev - m_next)
        p = jnp.exp2(s - jnp.concatenate([m_next] * (width // LANES), axis=1))
        psum = p[:, :LANES]
        for c in range(1, width // LANES):
            psum = psum + p[:, c * LANES:(c + 1) * LANES]
        l_sc[...] = alpha * l_sc[...] + psum
        acc_sc[...] = alpha * acc_sc[...] + jnp.dot(p.astype(BF16), v, preferred_element_type=F32)
        m_sc[...] = m_next

    def body(j, carry):
        step(pl.multiple_of(j * tk, tk), tk, False)
        return carry

    lax.fori_loop(0, i, body, 0)
    step(pl.multiple_of(i * tk, tk), tk, True)

    lam = (jnp.exp(jnp.sum(lq1_ref[...] * lk1_ref[...], axis=-1, keepdims=True))
           - jnp.exp(jnp.sum(lq2_ref[...] * lk2_ref[...], axis=-1, keepdims=True))
           + lam_init)
    o = acc_sc[...] / jnp.sum(l_sc[...], axis=1, keepdims=True)
    d = o[:tq, :] - lam * o[tq:, :]
    ms = jnp.mean(d * d, axis=-1, keepdims=True)
    out = d * lax.rsqrt(ms + RMS_EPS) * g_ref[...] * (1.0 - lam_init)
    o_ref[...] = out.astype(BF16)


def _diff_attention(proj, lq1, lk1, lq2, lk2, g, batch, seq, lam_init):
    n = batch * seq
    tq = ATT_TQ
    assert ATT_TQ == ATT_TK and seq % tq == 0 and DIFF_V_DIM == LANES
    nq = seq // tq
    qcol = COL_DQ // LANES
    kcol = COL_DK // LANES
    vcol = COL_DV // LANES
    vec = lambda b, h, i: (0, 0)
    return pl.pallas_call(
        functools.partial(_diff_attn_kernel, lam_init=lam_init),
        grid=(batch, N_DIFF_HEADS, nq),
        in_specs=[pl.BlockSpec((1, DIFF_QK_DIM), vec)] * 4 + [
            pl.BlockSpec((1, DIFF_V_DIM), vec),
            pl.BlockSpec((tq, LANES), lambda b, h, i: (b * nq + i, qcol + h)),
            pl.BlockSpec((seq, LANES), lambda b, h, i: (b, kcol + h)),
            pl.BlockSpec((seq, LANES), lambda b, h, i: (b, vcol + h))],
        out_specs=pl.BlockSpec((tq, LANES), lambda b, h, i: (b * nq + i, h)),
        out_shape=jax.ShapeDtypeStruct((n, DIFF_WIDTH), BF16),
        scratch_shapes=[pltpu.VMEM((2 * tq, LANES), BF16),
                        pltpu.VMEM((2 * tq, LANES), F32),
                        pltpu.VMEM((2 * tq, LANES), F32),
                        pltpu.VMEM((2 * tq, DIFF_V_DIM), F32)],
        compiler_params=_cparams("arbitrary", "arbitrary", "arbitrary"),
        name="diff_attn",
    )(lq1.reshape(1, -1), lk1.reshape(1, -1), lq2.reshape(1, -1), lk2.reshape(1, -1),
      g.reshape(1, -1), proj, proj, proj)


def _retention_kernel(q_ref, k_ref, v_ref, gate_ref, g_ref, o_ref, state_sc):
    c_len = RET_C

    @pl.when(pl.program_id(1) == 0)
    def _():
        state_sc[...] = jnp.zeros(state_sc.shape, F32)

    ii = lax.broadcasted_iota(I32, (c_len, c_len), 0)
    jj = lax.broadcasted_iota(I32, (c_len, c_len), 1)
    rel = (ii - jj).astype(F32)
    lane = lax.broadcasted_iota(I32, (c_len, LANES), 1)
    pos = lax.broadcasted_iota(I32, (c_len, LANES), 0).astype(F32)
    srow = lax.broadcasted_iota(I32, (LANES, LANES), 0)
    for pair in range(N_RET_HEADS // 2):
        ld = (LOG_DECAY[2 * pair], LOG_DECAY[2 * pair + 1])
        ld_lane = jnp.where(lane < RET_QK_DIM, ld[0], ld[1])
        q_decay = jnp.exp(ld_lane * (pos + 1.0))
        k_decay = jnp.exp(ld_lane * (c_len - 1.0 - pos))
        chunk_decay = jnp.where(srow < RET_QK_DIM, math.exp(ld[0] * c_len), math.exp(ld[1] * c_len))
        intra = [jnp.where(rel >= 0, jnp.exp(l * jnp.maximum(rel, 0.0)), 0.0) for l in ld]
        in_head = (lane < RET_QK_DIM, lane >= RET_QK_DIM)
        for c in range(RET_T // c_len):
            rows = slice(c * c_len, (c + 1) * c_len)
            qb = q_ref[rows, pair * LANES:(pair + 1) * LANES]
            kb = k_ref[rows, pair * LANES:(pair + 1) * LANES]
            q = qb.astype(F32)
            state = state_sc[pair]
            state_b = state.astype(BF16)
            kd_t = (kb.astype(F32) * k_decay).T.astype(BF16)
            new_kv = []
            for hh in range(2):
                h = 2 * pair + hh
                qm = jnp.where(in_head[hh], q, 0.0)
                s = lax.dot_general(qm.astype(BF16), kb, (((1,), (1,)), ((), ())),
                                    preferred_element_type=F32) * intra[hh]
                v = v_ref[rows, h * RET_V_DIM:(h + 1) * RET_V_DIM]
                y = (jnp.dot(s.astype(BF16), v, preferred_element_type=F32)
                     + jnp.dot((qm * q_decay).astype(BF16), state_b, preferred_element_type=F32))
                new_kv.append(jnp.dot(kd_t, v, preferred_element_type=F32))
                ms = jnp.mean(y * y, axis=-1, keepdims=True)
                yn = y * lax.rsqrt(ms + RMS_EPS) * g_ref[h:h + 1, :]
                gate = gate_ref[rows, h * RET_V_DIM:(h + 1) * RET_V_DIM].astype(F32)
                o_ref[rows, h * RET_V_DIM:(h + 1) * RET_V_DIM] = (yn * gate).astype(BF16)
            state_sc[pair] = chunk_decay * state + jnp.where(srow < RET_QK_DIM, new_kv[0], new_kv[1])


def _retention(proj, g, batch, seq):
    n = batch * seq
    t = RET_T
    nt = seq // t
    qk_w = N_RET_HEADS * RET_QK_DIM
    return pl.pallas_call(
        _retention_kernel,
        grid=(batch, nt),
        in_specs=[pl.BlockSpec((t, qk_w), lambda b, i: (b * nt + i, COL_RQ // qk_w)),
                  pl.BlockSpec((t, qk_w), lambda b, i: (b * nt + i, COL_RK // qk_w)),
                  pl.BlockSpec((t, RET_WIDTH), lambda b, i: (b * nt + i, COL_RV // RET_WIDTH)),
                  pl.BlockSpec((t, RET_WIDTH), lambda b, i: (b * nt + i, COL_RG // RET_WIDTH)),
                  pl.BlockSpec((N_RET_HEADS, RET_V_DIM), lambda b, i: (0, 0))],
        out_specs=pl.BlockSpec((t, RET_WIDTH), lambda b, i: (b * nt + i, 0)),
        out_shape=jax.ShapeDtypeStruct((n, RET_WIDTH), BF16),
        scratch_shapes=[pltpu.VMEM((N_RET_HEADS // 2, LANES, RET_V_DIM), F32)],
        compiler_params=_cparams("arbitrary", "arbitrary"),
        name="retention",
    )(proj, proj, proj, proj, g)


def _outproj_kernel(d_ref, r_ref, wo_ref, x_ref, g2_ref, wr_ref, br_ref,
                    x1_ref, h2p_ref, idx_ref, gate_ref):
    acc = (jnp.dot(d_ref[...], wo_ref[:DIFF_WIDTH, :], preferred_element_type=F32)
           + jnp.dot(r_ref[...], wo_ref[DIFF_WIDTH:, :], preferred_element_type=F32))
    x1 = x_ref[...] + acc
    x1_ref[...] = x1
    ms = jnp.mean(x1 * x1, axis=-1, keepdims=True)
    h2 = x1 * lax.rsqrt(ms + RMS_EPS) * g2_ref[...]
    half = h2.shape[1] // 2
    h2p_ref[...] = pltpu.pack_elementwise([h2[:, :half], h2[:, half:]], packed_dtype=BF16)

    logits = lax.dot_general(wr_ref[...], h2, (((1,), (1,)), ((), ())),
                             precision=lax.Precision.HIGHEST,
                             preferred_element_type=F32) + br_ref[...]
    e_iota = lax.broadcasted_iota(I32, logits.shape, 0)
    vals = []
    for r in range(TOP_K):
        m = jnp.max(logits, axis=0, keepdims=True)
        ix = jnp.min(jnp.where(logits == m, e_iota, N_EXPERTS), axis=0, keepdims=True)
        vals.append(m)
        idx_ref[r:r + 1, :] = ix
        logits = jnp.where(e_iota == ix, -jnp.inf, logits)
    ex = [jnp.exp(v - vals[0]) for v in vals]
    den = ex[0] + ex[1] + ex[2] + ex[3]
    for r in range(TOP_K):
        gate_ref[r:r + 1, :] = ex[r] / den


def _out_proj(d_out, r_out, w_o_b, x2d, g2, w_router_t, b_router):
    n, d = x2d.shape
    tm = OUT_TM
    const = lambda i: (0, 0)
    return pl.pallas_call(
        _outproj_kernel,
        grid=(n // tm,),
        in_specs=[pl.BlockSpec((tm, DIFF_WIDTH), lambda i: (i, 0)),
                  pl.BlockSpec((tm, RET_WIDTH), lambda i: (i, 0)),
                  pl.BlockSpec((DIFF_WIDTH + RET_WIDTH, d), const),
                  pl.BlockSpec((tm, d), lambda i: (i, 0)),
                  pl.BlockSpec((1, d), const),
                  pl.BlockSpec((N_EXPERTS, d), const),
                  pl.BlockSpec((N_EXPERTS, 1), const)],
        out_specs=[pl.BlockSpec((tm, d), lambda i: (i, 0)),
                   pl.BlockSpec((tm, d // 2), lambda i: (i, 0)),
                   pl.BlockSpec((TOP_K, tm), lambda i: (0, i)),
                   pl.BlockSpec((TOP_K, tm), lambda i: (0, i))],
        out_shape=[jax.ShapeDtypeStruct((n, d), F32),
                   jax.ShapeDtypeStruct((n, d // 2), U32),
                   jax.ShapeDtypeStruct((TOP_K, n), I32),
                   jax.ShapeDtypeStruct((TOP_K, n), F32)],
        compiler_params=_cparams("arbitrary"),
        name="out_proj",
    )(d_out, r_out, w_o_b, x2d, g2.reshape(1, d), w_router_t, b_router.reshape(N_EXPERTS, 1))


def _plan_kernel(idx_ref, dest_ref, blk_ref, ends_ref, cnt_sc, base_sc, tri_sc, *, nb_pad):
    ph = pl.program_id(0)
    c = pl.program_id(1)
    tc = PLAN_TC
    e_iota = lax.broadcasted_iota(I32, (N_EXPERTS, tc), 0)

    @pl.when((ph == 0) & (c == 0))
    def _():
        cnt_sc[...] = jnp.zeros(cnt_sc.shape, F32)
        s = lax.broadcasted_iota(I32, (tc, tc), 0)
        t = lax.broadcasted_iota(I32, (tc, tc), 1)
        tri_sc[...] = jnp.where(s < t, 1.0, 0.0).astype(BF16)

    @pl.when(ph == 0)
    def _():
        tot = jnp.zeros((N_EXPERTS, 1), F32)
        for k in range(TOP_K):
            oh = idx_ref[k:k + 1, :] == e_iota
            tot = tot + jnp.sum(jnp.where(oh, 1.0, 0.0), axis=1, keepdims=True)
        cnt_sc[...] = cnt_sc[...] + tot

    @pl.when((ph == 1) & (c == 0))
    def _():
        cnt = cnt_sc[...]
        nblk = jnp.floor((cnt + (MOE_BM - 1.0)) * (1.0 / MOE_BM))
        ei = lax.broadcasted_iota(I32, (N_EXPERTS, LANES), 0)
        li = lax.broadcasted_iota(I32, (N_EXPERTS, LANES), 1)
        nblk_row = jnp.sum(jnp.where(ei == li, nblk, 0.0), axis=0, keepdims=True)
        start = jnp.sum(jnp.where(li < ei, nblk_row, 0.0), axis=1, keepdims=True)
        base_sc[...] = start * MOE_BM
        end = start + nblk
        bi = lax.broadcasted_iota(I32, (N_EXPERTS, nb_pad), 1).astype(F32)
        be = jnp.sum(jnp.where(end <= bi, 1.0, 0.0), axis=0, keepdims=True)
        blk_ref[...] = jnp.minimum(be, N_EXPERTS - 1.0).astype(I32)
        ends_ref[...] = jnp.sum(jnp.where(ei == li, end, 0.0), axis=0, keepdims=True).astype(I32)

    @pl.when(ph == 1)
    def _():
        base = base_sc[...]
        for k in range(TOP_K):
            oh = idx_ref[k:k + 1, :] == e_iota
            ohb = jnp.where(oh, 1.0, 0.0).astype(BF16)
            before = jnp.dot(ohb, tri_sc[...], preferred_element_type=F32)
            rank = jnp.sum(jnp.where(oh, before + base, 0.0), axis=0, keepdims=True)
            dest_ref[k:k + 1, :] = rank.astype(I32)
            base = base + jnp.sum(jnp.where(oh, 1.0, 0.0), axis=1, keepdims=True)
        base_sc[...] = base


def _plan(top_idx, nb_pad):
    n = top_idx.shape[1]
    tc = PLAN_TC
    return pl.pallas_call(
        functools.partial(_plan_kernel, nb_pad=nb_pad),
        grid=(2, n // tc),
        in_specs=[pl.BlockSpec((TOP_K, tc), lambda ph, c: (0, c))],
        out_specs=[pl.BlockSpec((TOP_K, tc), lambda ph, c: (0, c * ph)),
                   pl.BlockSpec((1, nb_pad), lambda ph, c: (0, 0)),
                   pl.BlockSpec((1, LANES), lambda ph, c: (0, 0))],
        out_shape=[jax.ShapeDtypeStruct((TOP_K, n), I32),
                   jax.ShapeDtypeStruct((1, nb_pad), I32),
                   jax.ShapeDtypeStruct((1, LANES), I32)],
        scratch_shapes=[pltpu.VMEM((N_EXPERTS, 1), F32),
                        pltpu.VMEM((N_EXPERTS, 1), F32),
                        pltpu.VMEM((tc, tc), BF16)],
        compiler_params=_cparams("arbitrary", "arbitrary"),
        name="plan",
    )(top_idx)


def _dispatch_kernel(ends_ref, dest_ref, h_ref, xbuf_ref, zbuf, sem, zsem, *, nb):
    tm = DISP_TM
    bm = MOE_BM

    @pl.when(pl.program_id(0) == 0)
    def _():
        zbuf[...] = jnp.zeros(zbuf.shape, zbuf.dtype)

        def zero_block(blk):
            return pltpu.make_async_copy(zbuf, xbuf_ref.at[pl.ds(pl.multiple_of(blk * bm, bm), bm)], zsem)

        def per_block(fn):
            for e in range(N_EXPERTS):
                end = ends_ref[e]
                first = ends_ref[e - 1] if e else 0

                @pl.when(end > first)
                def _():
                    fn(zero_block(end - 1))

            def tail(blk, carry):
                fn(zero_block(blk))
                return carry

            lax.fori_loop(ends_ref[N_EXPERTS - 1], nb, tail, 0)

        per_block(lambda cp: cp.start())
        per_block(lambda cp: cp.wait())

    def issue(r, carry):
        for u in range(TOK_PER_ROW):
            for k in range(TOP_K):
                src = h_ref.at[r * (TOK_PER_ROW // SUBLANES) + u // SUBLANES, pl.ds(u % SUBLANES, 1)]
                dst = xbuf_ref.at[pl.ds(dest_ref[r, u * TOP_K + k], 1)]
                pltpu.make_async_copy(src, dst, sem).start(priority=k % 2)
        return carry

    lax.fori_loop(0, tm // TOK_PER_ROW, issue, 0)
    for k in range(TOP_K):
        pltpu.make_async_copy(xbuf_ref.at[pl.ds(0, tm)], xbuf_ref.at[pl.ds(0, tm)], sem).wait()


def _dispatch(ends, dest, h2p, p_rows):
    n, w = h2p.shape
    tm = DISP_TM
    grid_spec = pltpu.PrefetchScalarGridSpec(
        num_scalar_prefetch=1,
        grid=(n // tm,),
        in_specs=[pl.BlockSpec((tm // TOK_PER_ROW, LANES), lambda i, ends: (i, 0), memory_space=pltpu.SMEM),
                  pl.BlockSpec((tm // SUBLANES, SUBLANES, w), lambda i, ends: (i, 0, 0))],
        out_specs=pl.BlockSpec(memory_space=pl.ANY),
        scratch_shapes=[pltpu.VMEM((MOE_BM, w), h2p.dtype),
                        pltpu.SemaphoreType.DMA(()),
                        pltpu.SemaphoreType.DMA(())],
    )
    return pl.pallas_call(
        functools.partial(_dispatch_kernel, nb=p_rows // MOE_BM),
        grid_spec=grid_spec,
        out_shape=jax.ShapeDtypeStruct((p_rows, w), h2p.dtype),
        compiler_params=pltpu.CompilerParams(dimension_semantics=("arbitrary",),
                                             vmem_limit_bytes=VMEM_LIMIT,
                                             has_side_effects=True),
        name="dispatch",
    )(ends, dest, h2p.reshape(n // SUBLANES, SUBLANES, w))


def _pair_perm():
    a = lax.broadcasted_iota(I32, (MXU_DIM, MXU_DIM), 0)
    b = lax.broadcasted_iota(I32, (MXU_DIM, MXU_DIM), 1)
    src = jnp.where(b < LANES, 2 * b, 2 * (b - LANES) + 1)
    return jnp.where(a == src, 1.0, 0.0).astype(BF16)


def _wprep_kernel(w_ref, o_ref):
    perm = _pair_perm()
    for c in range(w_ref.shape[2] // MXU_DIM):
        cols = slice(c * MXU_DIM, (c + 1) * MXU_DIM)
        blk = w_ref[0, :, cols].astype(BF16)
        o_ref[0, :, cols] = jnp.dot(blk, perm, preferred_element_type=F32).astype(BF16)


def _prep_expert_in_weights(w):
    e, d, f2 = w.shape
    td = WPREP_TD
    return pl.pallas_call(
        _wprep_kernel,
        grid=(e, d // td),
        in_specs=[pl.BlockSpec((1, td, f2), lambda i, j: (i, j, 0))],
        out_specs=pl.BlockSpec((1, td, f2), lambda i, j: (i, j, 0)),
        out_shape=jax.ShapeDtypeStruct((e, d, f2), BF16),
        compiler_params=_cparams("arbitrary", "arbitrary"),
        name="expert_w_prep",
    )(w)


def _pair_group(b):
    e, f2 = b.shape
    return b.reshape(e, f2 // MXU_DIM, LANES, 2).transpose(0, 1, 3, 2).reshape(e, 1, f2)


def _expert_kernel(blk_ref, ends_ref, x_ref, w1_ref, b1_ref, w2_ref, b2_ref, y_ref):
    b = pl.program_id(0)
    n_used = ends_ref[N_EXPERTS - 1]

    @pl.when(b >= n_used)
    def _():
        y_ref[...] = jnp.zeros(y_ref.shape, y_ref.dtype)

    @pl.when(b < n_used)
    def _():
        w = x_ref[...]
        lo = pltpu.unpack_elementwise(w, index=0, packed_dtype=BF16, unpacked_dtype=F32)
        hi = pltpu.unpack_elementwise(w, index=1, packed_dtype=BF16, unpacked_dtype=F32)
        x = jnp.concatenate([lo.astype(BF16), hi.astype(BF16)], axis=1)
        acc = jnp.zeros((x.shape[0], w2_ref.shape[2]), F32)
        grp = 2 * MXU_DIM
        for c in range(w1_ref.shape[2] // grp):
            cols = slice(c * grp, (c + 1) * grp)
            h = jnp.dot(x, w1_ref[0, :, cols], preferred_element_type=F32) + b1_ref[0, :, cols]
            glu = jnp.concatenate([h[:, 0:LANES], h[:, 2 * LANES:3 * LANES]], axis=1)
            lin = jnp.concatenate([h[:, LANES:2 * LANES], h[:, 3 * LANES:4 * LANES]], axis=1)
            glu = jnp.minimum(glu, SWIGLU_LIMIT)
            lin = jnp.clip(lin, -SWIGLU_LIMIT, SWIGLU_LIMIT)
            act = glu * jax.nn.sigmoid(SWIGLU_ALPHA * glu) * (lin + 1.0)
            acc = acc + jnp.dot(act.astype(BF16), w2_ref[0, c * MXU_DIM:(c + 1) * MXU_DIM, :],
                                preferred_element_type=F32)
        y_ref[...] = acc + b2_ref[0]


def _experts(blk_expert, ends, x_buf, w1, b1, w2, b2):
    p_rows, half = x_buf.shape
    e, d, f2 = w1.shape
    f = f2 // 2
    bm = MOE_BM
    nb = p_rows // bm

    def row_blk(b, ends):
        return jnp.minimum(b, ends[N_EXPERTS - 1] - 1)

    grid_spec = pltpu.PrefetchScalarGridSpec(
        num_scalar_prefetch=2,
        grid=(nb,),
        in_specs=[pl.BlockSpec((bm, half), lambda b, blk, ends: (row_blk(b, ends), 0)),
                  pl.BlockSpec((1, d, f2), lambda b, blk, ends: (blk[row_blk(b, ends)], 0, 0)),
                  pl.BlockSpec((1, 1, f2), lambda b, blk, ends: (blk[row_blk(b, ends)], 0, 0)),
                  pl.BlockSpec((1, f, d), lambda b, blk, ends: (blk[row_blk(b, ends)], 0, 0)),
                  pl.BlockSpec((1, 1, d), lambda b, blk, ends: (blk[row_blk(b, ends)], 0, 0))],
        out_specs=pl.BlockSpec((bm, d), lambda b, blk, ends: (b, 0)),
    )
    return pl.pallas_call(
        _expert_kernel,
        grid_spec=grid_spec,
        out_shape=jax.ShapeDtypeStruct((p_rows, d), F32),
        compiler_params=_cparams("arbitrary"),
        name="experts",
    )(blk_expert, ends, x_buf, w1, b1, w2, b2)


def _combine_kernel(dest_ref, gate_ref, x1_ref, gf_ref, ybuf_ref, o_ref, gbuf, sem):
    tm = COMB_TM

    def issue(r, carry):
        for u in range(TOK_PER_ROW):
            for k in range(TOP_K):
                src = ybuf_ref.at[pl.ds(dest_ref[r, u * TOP_K + k], 1)]
                dst = gbuf.at[k, r * (TOK_PER_ROW // SUBLANES) + u // SUBLANES, pl.ds(u % SUBLANES, 1)]
                pltpu.make_async_copy(src, dst, sem).start(priority=k % 2)
        return carry

    lax.fori_loop(0, tm // TOK_PER_ROW, issue, 0)
    for k in range(TOP_K):
        pltpu.make_async_copy(ybuf_ref.at[pl.ds(0, tm)], ybuf_ref.at[pl.ds(0, tm)], sem).wait()

    gates = gate_ref[...]
    pad = jnp.zeros((LANES - TOP_K, LANES), F32)
    cols = []
    for c in range(tm // LANES):
        blk = jnp.concatenate([gates[:, c * LANES:(c + 1) * LANES], pad], axis=0)
        cols.append(blk.T)
    gcol = jnp.concatenate(cols, axis=0)
    acc = x1_ref[...]
    for k in range(TOP_K):
        acc = acc + gbuf[k].reshape(acc.shape) * gcol[:, k:k + 1]
    ms = jnp.mean(acc * acc, axis=-1, keepdims=True)
    o_ref[...] = acc * lax.rsqrt(ms + RMS_EPS) * gf_ref[...]


def _combine(dest, gates, x1, gf, y_buf):
    n, d = x1.shape
    tm = COMB_TM
    return pl.pallas_call(
        _combine_kernel,
        grid=(n // tm,),
        in_specs=[pl.BlockSpec((tm // TOK_PER_ROW, LANES), lambda i: (i, 0), memory_space=pltpu.SMEM),
                  pl.BlockSpec((TOP_K, tm), lambda i: (0, i)),
                  pl.BlockSpec((tm, d), lambda i: (i, 0)),
                  pl.BlockSpec((1, d), lambda i: (0, 0)),
                  pl.BlockSpec(memory_space=pl.ANY)],
        out_specs=pl.BlockSpec((tm, d), lambda i: (i, 0)),
        out_shape=jax.ShapeDtypeStruct((n, d), F32),
        scratch_shapes=[pltpu.VMEM((TOP_K, tm // SUBLANES, SUBLANES, d), F32),
                        pltpu.SemaphoreType.DMA(())],
        compiler_params=_cparams("arbitrary"),
        name="combine",
    )(dest, gates, x1, gf.reshape(1, d), y_buf)


def kernel(x, positions, norm1_g, w_in, lambda_q1, lambda_k1, lambda_q2, lambda_k2, diff_norm_g, ret_norm_g, w_o, norm2_g, w_router, b_router, w_moe_in, b_moe_in, w_moe_out, b_moe_out, norm_f_g):
    batch, seq, d = x.shape
    n = batch * seq
    assert norm1_g.shape[0] == 1, "single-layer block"
    l = 0
    p_rows = n * TOP_K + N_EXPERTS * MOE_BM
    nb = p_rows // MOE_BM
    nb_pad = -(-nb // LANES) * LANES

    cos, sin = _rope_tables(positions)
    x2d = x.reshape(n, d)
    lam_init = 0.8 - 0.6 * math.exp(-0.3 * l)
    proj = _in_proj(x2d, norm1_g[l], w_in[l].astype(BF16), cos, sin)
    d_out = _diff_attention(proj, lambda_q1[l], lambda_k1[l], lambda_q2[l], lambda_k2[l],
                            diff_norm_g[l], batch, seq, lam_init)
    r_out = _retention(proj, ret_norm_g[l], batch, seq)
    x1, h2p, top_idx, gates = _out_proj(d_out, r_out, w_o[l].astype(BF16), x2d, norm2_g[l],
                                        w_router[l].T, b_router[l])
    dest, blk_expert, ends = _plan(top_idx, nb_pad)
    ends = ends.reshape(-1)
    dest = dest.T.reshape(n // TOK_PER_ROW, LANES)
    x_buf = _dispatch(ends, dest, h2p, p_rows)
    w1 = _prep_expert_in_weights(w_moe_in[l])
    b1 = _pair_group(b_moe_in[l])
    w2 = w_moe_out[l].astype(BF16)
    b2 = b_moe_out[l].reshape(N_EXPERTS, 1, -1)
    y_buf = _experts(blk_expert.reshape(-1), ends, x_buf, w1, b1, w2, b2)
    out = _combine(dest, gates, x1, norm_f_g, y_buf)
    return out.reshape(batch, seq, d)
```

```python
import functools
import math

import numpy as np
import jax
import jax.numpy as jnp
from jax import lax
from jax.experimental import pallas as pl
from jax.experimental.pallas import tpu as pltpu

F32 = jnp.float32
BF16 = jnp.bfloat16
I32 = jnp.int32
U32 = jnp.uint32

N_DIFF_HEADS = 4
DIFF_QK_DIM = 64
DIFF_V_DIM = 128
DIFF_WIDTH = N_DIFF_HEADS * DIFF_V_DIM
N_RET_HEADS = 4
RET_QK_DIM = 64
RET_V_DIM = 128
RET_WIDTH = N_RET_HEADS * RET_V_DIM
ROPE_THETA = 10000.0
RMS_EPS = 1e-5
N_EXPERTS = 32
TOP_K = 4
SWIGLU_LIMIT = 7.0
SWIGLU_ALPHA = 1.702
TOK_PER_ROW = 128 // TOP_K
COL_DQ = 0
COL_DK = 512
COL_DV = 1024
COL_RQ = 1536
COL_RK = 1792
COL_RV = 2048
COL_RG = 2560
PROJ_WIDTH = 3072
LOG_DECAY = tuple(math.log(1.0 - 2.0 ** (-5.0 - h)) for h in range(N_RET_HEADS))
LOG2E = 1.4426950408889634

LANES = 128
SUBLANES = 8
MXU_DIM = 256
VMEM_LIMIT = 56 * 1024 * 1024

ROPE_ROWS = 1024
PROJ_TM = 512
ATT_TQ = 1024
ATT_TK = 1024
RET_T = 512
RET_C = 256
OUT_TM = 512
PLAN_TC = 512
DISP_TM = 1024
MOE_BM = 512
COMB_TM = 512
COMB_CH = 64

NEG_BIG = -1e30


def _cparams(*sem):
    return pltpu.CompilerParams(dimension_semantics=sem, vmem_limit_bytes=VMEM_LIMIT)


def _rope_table_kernel(pos_ref, invf_ref, cos_ref, sin_ref):
    invf = invf_ref[...]
    lane = lax.broadcasted_iota(I32, (LANES, LANES), 1)
    first_half = (lane & 32) == 0
    for r in range(ROPE_ROWS // LANES):
        prow = pos_ref[r:r + 1, :].astype(F32)
        pcol = jnp.broadcast_to(prow, (LANES, LANES)).T
        ang = pcol * invf
        s = jnp.sin(ang)
        cos_ref[r * LANES:(r + 1) * LANES, :] = jnp.cos(ang)
        sin_ref[r * LANES:(r + 1) * LANES, :] = jnp.where(first_half, -s, s)


def _rope_tables(positions):
    n = positions.size
    pos2d = positions.reshape(n // LANES, LANES)
    d = DIFF_QK_DIM
    inv_freq = 1.0 / (ROPE_THETA ** (jnp.arange(0, d, 2, dtype=F32) / d))
    invf = jnp.tile(inv_freq, LANES // (d // 2)).reshape(1, LANES)
    rows = ROPE_ROWS // LANES
    return pl.pallas_call(
        _rope_table_kernel,
        grid=(n // ROPE_ROWS,),
        in_specs=[pl.BlockSpec((rows, LANES), lambda i: (i, 0)),
                  pl.BlockSpec((1, LANES), lambda i: (0, 0))],
        out_specs=[pl.BlockSpec((ROPE_ROWS, LANES), lambda i: (i, 0)),
                   pl.BlockSpec((ROPE_ROWS, LANES), lambda i: (i, 0))],
        out_shape=[jax.ShapeDtypeStruct((n, LANES), F32)] * 2,
        compiler_params=_cparams("arbitrary"),
        name="rope_table",
    )(pos2d, invf)


def _inproj_kernel(x_ref, g_ref, w_ref, cos_ref, sin_ref, o_ref):
    x = x_ref[...]
    ms = jnp.mean(x * x, axis=-1, keepdims=True)
    h = (x * lax.rsqrt(ms + RMS_EPS) * g_ref[...]).astype(BF16)
    cos = cos_ref[...]
    sin = sin_ref[...]
    lane = lax.broadcasted_iota(I32, cos.shape, 1)
    first_half = (lane & 32) == 0

    def rope(t):
        rot = jnp.where(first_half, pltpu.roll(t, 96, 1), pltpu.roll(t, 32, 1))
        return t * cos + rot * sin

    q_scale = DIFF_QK_DIM ** -0.5 * LOG2E
    k_scale = RET_QK_DIM ** -0.5
    for c in range(PROJ_WIDTH // MXU_DIM):
        p = jnp.dot(h, w_ref[:, c * MXU_DIM:(c + 1) * MXU_DIM], preferred_element_type=F32)
        for half in range(MXU_DIM // LANES):
            col = c * MXU_DIM + half * LANES
            t = p[:, half * LANES:(half + 1) * LANES]
            if col < COL_DK:
                t = rope(t) * q_scale
            elif col < COL_DV:
                t = rope(t)
            elif col < COL_RQ:
                pass
            elif col < COL_RK:
                t = rope(t)
            elif col < COL_RV:
                t = rope(t) * k_scale
            elif col < COL_RG:
                pass
            else:
                t = t * jax.nn.sigmoid(t)
            o_ref[:, col:col + LANES] = t.astype(BF16)


def _in_proj(x2d, g1, w_in_b, cos, sin):
    n, d = x2d.shape
    tm = PROJ_TM
    return pl.pallas_call(
        _inproj_kernel,
        grid=(n // tm,),
        in_specs=[pl.BlockSpec((tm, d), lambda i: (i, 0)),
                  pl.BlockSpec((1, d), lambda i: (0, 0)),
                  pl.BlockSpec((d, PROJ_WIDTH), lambda i: (0, 0)),
                  pl.BlockSpec((tm, LANES), lambda i: (i, 0)),
                  pl.BlockSpec((tm, LANES), lambda i: (i, 0))],
        out_specs=pl.BlockSpec((tm, PROJ_WIDTH), lambda i: (i, 0)),
        out_shape=jax.ShapeDtypeStruct((n, PROJ_WIDTH), BF16),
        compiler_params=_cparams("arbitrary"),
        name="in_proj",
    )(x2d, g1.reshape(1, d), w_in_b, cos, sin)


def _diff_attn_kernel(lq1_ref, lk1_ref, lq2_ref, lk2_ref, g_ref, q_ref, k_ref, v_ref, o_ref,
                      qq_sc, m_sc, l_sc, acc_sc, *, lam_init):
    tq, tk = ATT_TQ, ATT_TK
    i = pl.program_id(2)
    q = q_ref[...]
    lane = lax.broadcasted_iota(I32, q.shape, 1)
    zero = jnp.zeros_like(q)
    qq_sc[:tq, :] = jnp.where(lane < DIFF_QK_DIM, q, zero)
    qq_sc[tq:, :] = jnp.where(lane >= DIFF_QK_DIM, q, zero)
    m_sc[...] = jnp.full(m_sc.shape, NEG_BIG, F32)
    l_sc[...] = jnp.zeros(l_sc.shape, F32)
    acc_sc[...] = jnp.zeros(acc_sc.shape, F32)

    def step(start, width, masked):
        k = k_ref[pl.ds(start, width), :]
        v = v_ref[pl.ds(start, width), :]
        s = lax.dot_general(qq_sc[...], k, (((1,), (1,)), ((), ())), preferred_element_type=F32)
        if masked:
            row = lax.broadcasted_iota(I32, s.shape, 0) & (tq - 1)
            col = lax.broadcasted_iota(I32, s.shape, 1)
            s = jnp.where(col <= row, s, NEG_BIG)
        m_prev = m_sc[...]
        m_next = jnp.maximum(m_prev, jnp.max(s, axis=1, keepdims=True))
        alpha = jnp.exp2(m_prev - m_next)
        p = jnp.exp2(s - jnp.concatenate([m_next] * (width // LANES), axis=1))
        psum = p[:, :LANES]
        for c in range(1, width // LANES):
            psum = psum + p[:, c * LANES:(c + 1) * LANES]
        l_sc[...] = alpha * l_sc[...] + psum
        acc_sc[...] = alpha * acc_sc[...] + jnp.dot(p.astype(BF16), v, preferred_element_type=F32)
        m_sc[...] = m_next

    def body(j, carry):
        step(pl.multiple_of(j * tk, tk), tk, False)
        return carry

    lax.fori_loop(0, i, body, 0)
    step(pl.multiple_of(i * tk, tk), tk, True)

    lam = (jnp.exp(jnp.sum(lq1_ref[...] * lk1_ref[...], axis=-1, keepdims=True))
           - jnp.exp(jnp.sum(lq2_ref[...] * lk2_ref[...], axis=-1, keepdims=True))
           + lam_init)
    o = acc_sc[...] / jnp.sum(l_sc[...], axis=1, keepdims=True)
    d = o[:tq, :] - lam * o[tq:, :]
    ms = jnp.mean(d * d, axis=-1, keepdims=True)
    out = d * lax.rsqrt(ms + RMS_EPS) * g_ref[...] * (1.0 - lam_init)
    o_ref[...] = out.astype(BF16)


def _diff_attention(proj, lq1, lk1, lq2, lk2, g, batch, seq, lam_init):
    n = batch * seq
    tq = ATT_TQ
    assert ATT_TQ == ATT_TK and seq % tq == 0 and DIFF_V_DIM == LANES
    nq = seq // tq
    qcol = COL_DQ // LANES
    kcol = COL_DK // LANES
    vcol = COL_DV // LANES
    vec = lambda b, h, i: (0, 0)
    return pl.pallas_call(
        functools.partial(_diff_attn_kernel, lam_init=lam_init),
        grid=(batch, N_DIFF_HEADS, nq),
        in_specs=[pl.BlockSpec((1, DIFF_QK_DIM), vec)] * 4 + [
            pl.BlockSpec((1, DIFF_V_DIM), vec),
            pl.BlockSpec((tq, LANES), lambda b, h, i: (b * nq + i, qcol + h)),
            pl.BlockSpec((seq, LANES), lambda b, h, i: (b, kcol + h)),
            pl.BlockSpec((seq, LANES), lambda b, h, i: (b, vcol + h))],
        out_specs=pl.BlockSpec((tq, LANES), lambda b, h, i: (b * nq + i, h)),
        out_shape=jax.ShapeDtypeStruct((n, DIFF_WIDTH), BF16),
        scratch_shapes=[pltpu.VMEM((2 * tq, LANES), BF16),
                        pltpu.VMEM((2 * tq, LANES), F32),
                        pltpu.VMEM((2 * tq, LANES), F32),
                        pltpu.VMEM((2 * tq, DIFF_V_DIM), F32)],
        compiler_params=_cparams("arbitrary", "arbitrary", "arbitrary"),
        name="diff_attn",
    )(lq1.reshape(1, -1), lk1.reshape(1, -1), lq2.reshape(1, -1), lk2.reshape(1, -1),
      g.reshape(1, -1), proj, proj, proj)


def _retention_kernel(q_ref, k_ref, v_ref, gate_ref, g_ref, o_ref, state_sc):
    c_len = RET_C

    @pl.when(pl.program_id(1) == 0)
    def _():
        state_sc[...] = jnp.zeros(state_sc.shape, F32)

    ii = lax.broadcasted_iota(I32, (c_len, c_len), 0)
    jj = lax.broadcasted_iota(I32, (c_len, c_len), 1)
    rel = (ii - jj).astype(F32)
    lane = lax.broadcasted_iota(I32, (c_len, LANES), 1)
    pos = lax.broadcasted_iota(I32, (c_len, LANES), 0).astype(F32)
    srow = lax.broadcasted_iota(I32, (LANES, LANES), 0)
    for pair in range(N_RET_HEADS // 2):
        ld = (LOG_DECAY[2 * pair], LOG_DECAY[2 * pair + 1])
        ld_lane = jnp.where(lane < RET_QK_DIM, ld[0], ld[1])
        q_decay = jnp.exp(ld_lane * (pos + 1.0))
        k_decay = jnp.exp(ld_lane * (c_len - 1.0 - pos))
        chunk_decay = jnp.where(srow < RET_QK_DIM, math.exp(ld[0] * c_len), math.exp(ld[1] * c_len))
        intra = [jnp.where(rel >= 0, jnp.exp(l * jnp.maximum(rel, 0.0)), 0.0) for l in ld]
        in_head = (lane < RET_QK_DIM, lane >= RET_QK_DIM)
        for c in range(RET_T // c_len):
            rows = slice(c * c_len, (c + 1) * c_len)
            qb = q_ref[rows, pair * LANES:(pair + 1) * LANES]
            kb = k_ref[rows, pair * LANES:(pair + 1) * LANES]
            q = qb.astype(F32)
            state = state_sc[pair]
            state_b = state.astype(BF16)
            kd_t = (kb.astype(F32) * k_decay).T.astype(BF16)
            new_kv = []
            for hh in range(2):
                h = 2 * pair + hh
                qm = jnp.where(in_head[hh], q, 0.0)
                s = lax.dot_general(qm.astype(BF16), kb, (((1,), (1,)), ((), ())),
                                    preferred_element_type=F32) * intra[hh]
                v = v_ref[rows, h * RET_V_DIM:(h + 1) * RET_V_DIM]
                y = (jnp.dot(s.astype(BF16), v, preferred_element_type=F32)
                     + jnp.dot((qm * q_decay).astype(BF16), state_b, preferred_element_type=F32))
                new_kv.append(jnp.dot(kd_t, v, preferred_element_type=F32))
                ms = jnp.mean(y * y, axis=-1, keepdims=True)
                yn = y * lax.rsqrt(ms + RMS_EPS) * g_ref[h:h + 1, :]
                gate = gate_ref[rows, h * RET_V_DIM:(h + 1) * RET_V_DIM].astype(F32)
                o_ref[rows, h * RET_V_DIM:(h + 1) * RET_V_DIM] = (yn * gate).astype(BF16)
            state_sc[pair] = chunk_decay * state + jnp.where(srow < RET_QK_DIM, new_kv[0], new_kv[1])


def _retention(proj, g, batch, seq):
    n = batch * seq
    t = RET_T
    nt = seq // t
    qk_w = N_RET_HEADS * RET_QK_DIM
    return pl.pallas_call(
        _retention_kernel,
        grid=(batch, nt),
        in_specs=[pl.BlockSpec((t, qk_w), lambda b, i: (b * nt + i, COL_RQ // qk_w)),
                  pl.BlockSpec((t, qk_w), lambda b, i: (b * nt + i, COL_RK // qk_w)),
                  pl.BlockSpec((t, RET_WIDTH), lambda b, i: (b * nt + i, COL_RV // RET_WIDTH)),
                  pl.BlockSpec((t, RET_WIDTH), lambda b, i: (b * nt + i, COL_RG // RET_WIDTH)),
                  pl.BlockSpec((N_RET_HEADS, RET_V_DIM), lambda b, i: (0, 0))],
        out_specs=pl.BlockSpec((t, RET_WIDTH), lambda b, i: (b * nt + i, 0)),
        out_shape=jax.ShapeDtypeStruct((n, RET_WIDTH), BF16),
        scratch_shapes=[pltpu.VMEM((N_RET_HEADS // 2, LANES, RET_V_DIM), F32)],
        compiler_params=_cparams("arbitrary", "arbitrary"),
        name="retention",
    )(proj, proj, proj, proj, g)


def _outproj_kernel(d_ref, r_ref, wo_ref, x_ref, g2_ref, wr_ref, br_ref,
                    x1_ref, h2p_ref, idx_ref, gate_ref):
    acc = (jnp.dot(d_ref[...], wo_ref[:DIFF_WIDTH, :], preferred_element_type=F32)
           + jnp.dot(r_ref[...], wo_ref[DIFF_WIDTH:, :], preferred_element_type=F32))
    x1 = x_ref[...] + acc
    x1_ref[...] = x1
    ms = jnp.mean(x1 * x1, axis=-1, keepdims=True)
    h2 = x1 * lax.rsqrt(ms + RMS_EPS) * g2_ref[...]
    half = h2.shape[1] // 2
    packed = pltpu.pack_elementwise([h2[:, :half], h2[:, half:]], packed_dtype=BF16)
    slabs = half // LANES
    for j in range(slabs):
        h2p_ref[pl.ds(j, packed.shape[0], stride=slabs), :] = packed[:, j * LANES:(j + 1) * LANES]

    nt = (((1,), (1,)), ((), ()))
    wr = wr_ref[...]
    wr_hi = wr.astype(BF16)
    wr_lo = (wr - wr_hi.astype(F32)).astype(BF16)
    h2_hi = h2.astype(BF16)
    h2_lo = (h2 - h2_hi.astype(F32)).astype(BF16)
    logits = (lax.dot_general(wr_hi, h2_hi, nt, preferred_element_type=F32)
              + lax.dot_general(wr_hi, h2_lo, nt, preferred_element_type=F32)
              + lax.dot_general(wr_lo, h2_hi, nt, preferred_element_type=F32)
              + br_ref[...])
    e_iota = lax.broadcasted_iota(I32, logits.shape, 0)
    vals = []
    for r in range(TOP_K):
        m = jnp.max(logits, axis=0, keepdims=True)
        ix = jnp.min(jnp.where(logits == m, e_iota, N_EXPERTS), axis=0, keepdims=True)
        vals.append(m)
        idx_ref[r:r + 1, :] = ix
        logits = jnp.where(e_iota == ix, -jnp.inf, logits)
    ex = [jnp.exp(v - vals[0]) for v in vals]
    den = ex[0] + ex[1] + ex[2] + ex[3]
    for r in range(TOP_K):
        gate_ref[r:r + 1, :] = ex[r] / den


def _out_proj(d_out, r_out, w_o_b, x2d, g2, w_router_t, b_router):
    n, d = x2d.shape
    tm = OUT_TM
    const = lambda i: (0, 0)
    return pl.pallas_call(
        _outproj_kernel,
        grid=(n // tm,),
        in_specs=[pl.BlockSpec((tm, DIFF_WIDTH), lambda i: (i, 0)),
                  pl.BlockSpec((tm, RET_WIDTH), lambda i: (i, 0)),
                  pl.BlockSpec((DIFF_WIDTH + RET_WIDTH, d), const),
                  pl.BlockSpec((tm, d), lambda i: (i, 0)),
                  pl.BlockSpec((1, d), const),
                  pl.BlockSpec((N_EXPERTS, d), const),
                  pl.BlockSpec((N_EXPERTS, 1), const)],
        out_specs=[pl.BlockSpec((tm, d), lambda i: (i, 0)),
                   pl.BlockSpec((tm * (d // 2 // LANES), LANES), lambda i: (i, 0)),
                   pl.BlockSpec((TOP_K, tm), lambda i: (0, i)),
                   pl.BlockSpec((TOP_K, tm), lambda i: (0, i))],
        out_shape=[jax.ShapeDtypeStruct((n, d), F32),
                   jax.ShapeDtypeStruct((n * (d // 2 // LANES), LANES), U32),
                   jax.ShapeDtypeStruct((TOP_K, n), I32),
                   jax.ShapeDtypeStruct((TOP_K, n), F32)],
        compiler_params=_cparams("arbitrary"),
        name="out_proj",
    )(d_out, r_out, w_o_b, x2d, g2.reshape(1, d), w_router_t, b_router.reshape(N_EXPERTS, 1))


def _plan_kernel(idx_ref, dest_ref, blk_ref, ends_ref, cnt_sc, base_sc, tri_sc, *, nb_pad):
    ph = pl.program_id(0)
    c = pl.program_id(1)
    tc = PLAN_TC
    e_iota = lax.broadcasted_iota(I32, (N_EXPERTS, tc), 0)

    @pl.when((ph == 0) & (c == 0))
    def _():
        cnt_sc[...] = jnp.zeros(cnt_sc.shape, F32)
        s = lax.broadcasted_iota(I32, (tc, tc), 0)
        t = lax.broadcasted_iota(I32, (tc, tc), 1)
        tri_sc[...] = jnp.where(s < t, 1.0, 0.0).astype(BF16)

    @pl.when(ph == 0)
    def _():
        tot = jnp.zeros((N_EXPERTS, 1), F32)
        for k in range(TOP_K):
            oh = idx_ref[k:k + 1, :] == e_iota
            tot = tot + jnp.sum(jnp.where(oh, 1.0, 0.0), axis=1, keepdims=True)
        cnt_sc[...] = cnt_sc[...] + tot

    @pl.when((ph == 1) & (c == 0))
    def _():
        cnt = cnt_sc[...]
        nblk = jnp.floor((cnt + (MOE_BM - 1.0)) * (1.0 / MOE_BM))
        ei = lax.broadcasted_iota(I32, (N_EXPERTS, LANES), 0)
        li = lax.broadcasted_iota(I32, (N_EXPERTS, LANES), 1)
        nblk_row = jnp.sum(jnp.where(ei == li, nblk, 0.0), axis=0, keepdims=True)
        start = jnp.sum(jnp.where(li < ei, nblk_row, 0.0), axis=1, keepdims=True)
        base_sc[...] = start * MOE_BM
        end = start + nblk
        bi = lax.broadcasted_iota(I32, (N_EXPERTS, nb_pad), 1).astype(F32)
        be = jnp.sum(jnp.where(end <= bi, 1.0, 0.0), axis=0, keepdims=True)
        blk_ref[...] = jnp.minimum(be, N_EXPERTS - 1.0).astype(I32)
        ends_ref[...] = jnp.sum(jnp.where(ei == li, end, 0.0), axis=0, keepdims=True).astype(I32)

    @pl.when(ph == 1)
    def _():
        base = base_sc[...]
        for k in range(TOP_K):
            oh = idx_ref[k:k + 1, :] == e_iota
            ohb = jnp.where(oh, 1.0, 0.0).astype(BF16)
            before = jnp.dot(ohb, tri_sc[...], preferred_element_type=F32)
            rank = jnp.sum(jnp.where(oh, before + base, 0.0), axis=0, keepdims=True)
            dest_ref[k:k + 1, :] = rank.astype(I32)
            base = base + jnp.sum(jnp.where(oh, 1.0, 0.0), axis=1, keepdims=True)
        base_sc[...] = base


def _plan(top_idx, nb_pad):
    n = top_idx.shape[1]
    tc = PLAN_TC
    return pl.pallas_call(
        functools.partial(_plan_kernel, nb_pad=nb_pad),
        grid=(2, n // tc),
        in_specs=[pl.BlockSpec((TOP_K, tc), lambda ph, c: (0, c))],
        out_specs=[pl.BlockSpec((TOP_K, tc), lambda ph, c: (0, c * ph)),
                   pl.BlockSpec((1, nb_pad), lambda ph, c: (0, 0)),
                   pl.BlockSpec((1, LANES), lambda ph, c: (0, 0))],
        out_shape=[jax.ShapeDtypeStruct((TOP_K, n), I32),
                   jax.ShapeDtypeStruct((1, nb_pad), I32),
                   jax.ShapeDtypeStruct((1, LANES), I32)],
        scratch_shapes=[pltpu.VMEM((N_EXPERTS, 1), F32),
                        pltpu.VMEM((N_EXPERTS, 1), F32),
                        pltpu.VMEM((tc, tc), BF16)],
        compiler_params=_cparams("arbitrary", "arbitrary"),
        name="plan",
    )(top_idx)


def _dispatch_kernel(ends_ref, dest_ref, h_ref, xbuf_ref, zbuf, sem, zsem, *, nb):
    tm = DISP_TM
    bm = MOE_BM

    @pl.when(pl.program_id(0) == 0)
    def _():
        zbuf[...] = jnp.zeros(zbuf.shape, zbuf.dtype)

        def zero_block(blk):
            return pltpu.make_async_copy(zbuf, xbuf_ref.at[pl.ds(pl.multiple_of(blk * bm, bm), bm)], zsem)

        def per_block(fn):
            for e in range(N_EXPERTS):
                end = ends_ref[e]
                first = ends_ref[e - 1] if e else 0

                @pl.when(end > first)
                def _():
                    fn(zero_block(end - 1))

            def tail(blk, carry):
                fn(zero_block(blk))
                return carry

            lax.fori_loop(ends_ref[N_EXPERTS - 1], nb, tail, 0)

        per_block(lambda cp: cp.start())
        per_block(lambda cp: cp.wait())

    def issue(r, carry):
        for u in range(TOK_PER_ROW):
            for k in range(TOP_K):
                src = h_ref.at[r * TOK_PER_ROW + u]
                dst = xbuf_ref.at[dest_ref[r, u * TOP_K + k]]
                pltpu.make_async_copy(src, dst, sem).start(priority=k % 2)
        return carry

    lax.fori_loop(0, tm // TOK_PER_ROW, issue, 0)
    for k in range(TOP_K):
        pltpu.make_async_copy(xbuf_ref.at[pl.ds(0, tm)], xbuf_ref.at[pl.ds(0, tm)], sem).wait()


def _dispatch(ends, dest, h2p, p_rows):
    n, s, w = h2p.shape
    tm = DISP_TM
    grid_spec = pltpu.PrefetchScalarGridSpec(
        num_scalar_prefetch=1,
        grid=(n // tm,),
        in_specs=[pl.BlockSpec((tm // TOK_PER_ROW, LANES), lambda i, ends: (i, 0), memory_space=pltpu.SMEM),
                  pl.BlockSpec((tm, s, w), lambda i, ends: (i, 0, 0))],
        out_specs=pl.BlockSpec(memory_space=pl.ANY),
        scratch_shapes=[pltpu.VMEM((MOE_BM, s, w), h2p.dtype),
                        pltpu.SemaphoreType.DMA(()),
                        pltpu.SemaphoreType.DMA(())],
    )
    return pl.pallas_call(
        functools.partial(_dispatch_kernel, nb=p_rows // MOE_BM),
        grid_spec=grid_spec,
        out_shape=jax.ShapeDtypeStruct((p_rows, s, w), h2p.dtype),
        compiler_params=pltpu.CompilerParams(dimension_semantics=("arbitrary",),
                                             vmem_limit_bytes=VMEM_LIMIT,
                                             has_side_effects=True),
        name="dispatch",
    )(ends, dest, h2p)


def _pair_perm():
    a = lax.broadcasted_iota(I32, (MXU_DIM, MXU_DIM), 0)
    b = lax.broadcasted_iota(I32, (MXU_DIM, MXU_DIM), 1)
    src = jnp.where(b < LANES, 2 * b, 2 * (b - LANES) + 1)
    return jnp.where(a == src, 1.0, 0.0).astype(BF16)


def _pair_group(b):
    e, f2 = b.shape
    return b.reshape(e, f2 // MXU_DIM, LANES, 2).transpose(0, 1, 3, 2).reshape(e, 1, f2)


def _expert_kernel(blk_ref, ends_ref, x_ref, w1f_ref, b1_ref, w2f_ref, b2_ref, y_ref, w1_ref, w2_ref):
    b = pl.program_id(0)
    n_used = ends_ref[N_EXPERTS - 1]

    @pl.when(b >= n_used)
    def _():
        y_ref[...] = jnp.zeros(y_ref.shape, y_ref.dtype)

    new_expert = (b == 0) | (blk_ref[b] != blk_ref[jnp.maximum(b - 1, 0)])

    @pl.when((b < n_used) & new_expert)
    def _():
        perm = _pair_perm()
        for c in range(w1f_ref.shape[2] // MXU_DIM):
            cols = slice(c * MXU_DIM, (c + 1) * MXU_DIM)
            blk = w1f_ref[0, :, cols].astype(BF16)
            w1_ref[0, :, cols] = jnp.dot(blk, perm, preferred_element_type=F32).astype(BF16)
        w2_ref[0] = w2f_ref[0].astype(BF16)

    @pl.when(b < n_used)
    def _():
        bm = y_ref.shape[0] // (w2_ref.shape[2] // LANES)
        ns = x_ref.shape[0] // bm
        slabs = [x_ref[pl.ds(j, bm, stride=ns), :] for j in range(ns)]
        lo = [pltpu.unpack_elementwise(w, index=0, packed_dtype=BF16, unpacked_dtype=F32).astype(BF16)
              for w in slabs]
        hi = [pltpu.unpack_elementwise(w, index=1, packed_dtype=BF16, unpacked_dtype=F32).astype(BF16)
              for w in slabs]
        x = jnp.concatenate(lo + hi, axis=1)
        acc = jnp.zeros((x.shape[0], w2_ref.shape[2]), F32)
        grp = 2 * MXU_DIM
        for c in range(w1_ref.shape[2] // grp):
            cols = slice(c * grp, (c + 1) * grp)
            h = jnp.dot(x, w1_ref[0, :, cols], preferred_element_type=F32) + b1_ref[0, :, cols]
            glu = jnp.concatenate([h[:, 0:LANES], h[:, 2 * LANES:3 * LANES]], axis=1)
            lin = jnp.concatenate([h[:, LANES:2 * LANES], h[:, 3 * LANES:4 * LANES]], axis=1)
            glu = jnp.minimum(glu, SWIGLU_LIMIT)
            lin = jnp.clip(lin, -SWIGLU_LIMIT, SWIGLU_LIMIT)
            act = glu * jax.nn.sigmoid(SWIGLU_ALPHA * glu) * (lin + 1.0)
            acc = acc + jnp.dot(act.astype(BF16), w2_ref[0, c * MXU_DIM:(c + 1) * MXU_DIM, :],
                                preferred_element_type=F32)
        y = acc + b2_ref[0]
        ys = y.shape[1] // LANES
        for j in range(ys):
            y_ref[pl.ds(j, bm, stride=ys), :] = y[:, j * LANES:(j + 1) * LANES]


def _experts(blk_expert, ends, x_buf, w1, b1, w2, b2):
    e, d, f2 = w1.shape
    xs = d // 2 // LANES
    ys = d // LANES
    p_rows = x_buf.shape[0] // xs
    f = f2 // 2
    bm = MOE_BM
    nb = p_rows // bm

    def row_blk(b, ends):
        return jnp.minimum(b, ends[N_EXPERTS - 1] - 1)

    grid_spec = pltpu.PrefetchScalarGridSpec(
        num_scalar_prefetch=2,
        grid=(nb,),
        in_specs=[pl.BlockSpec((bm * xs, LANES), lambda b, blk, ends: (row_blk(b, ends), 0)),
                  pl.BlockSpec((1, d, f2), lambda b, blk, ends: (blk[row_blk(b, ends)], 0, 0)),
                  pl.BlockSpec((1, 1, f2), lambda b, blk, ends: (blk[row_blk(b, ends)], 0, 0)),
                  pl.BlockSpec((1, f, d), lambda b, blk, ends: (blk[row_blk(b, ends)], 0, 0)),
                  pl.BlockSpec((1, 1, d), lambda b, blk, ends: (blk[row_blk(b, ends)], 0, 0))],
        out_specs=pl.BlockSpec((bm * ys, LANES), lambda b, blk, ends: (b, 0)),
        scratch_shapes=[pltpu.VMEM((1, d, f2), BF16),
                        pltpu.VMEM((1, f, d), BF16)],
    )
    return pl.pallas_call(
        _expert_kernel,
        grid_spec=grid_spec,
        out_shape=jax.ShapeDtypeStruct((p_rows * ys, LANES), F32),
        compiler_params=_cparams("arbitrary"),
        name="experts",
    )(blk_expert, ends, x_buf, w1, b1, w2, b2)


def _combine_kernel(dest_ref, gate_ref, x1_ref, gf_ref, ybuf_ref, o_ref, gbuf, sem):
    tm = COMB_TM

    ys = ybuf_ref.shape[1]

    def issue(r, carry):
        for u in range(TOK_PER_ROW):
            for k in range(TOP_K):
                src = ybuf_ref.at[dest_ref[r, u * TOP_K + k]]
                dst = gbuf.at[k, :, pl.ds(r * TOK_PER_ROW + u, 1), :]
                pltpu.make_async_copy(src, dst, sem).start(priority=k % 2)
        return carry

    lax.fori_loop(0, tm // TOK_PER_ROW, issue, 0)
    for k in range(TOP_K):
        pltpu.make_async_copy(ybuf_ref.at[pl.ds(0, tm)], ybuf_ref.at[pl.ds(0, tm)], sem).wait()

    gates = gate_ref[...]
    pad = jnp.zeros((LANES - TOP_K, LANES), F32)
    cols = []
    for c in range(tm // LANES):
        blk = jnp.concatenate([gates[:, c * LANES:(c + 1) * LANES], pad], axis=0)
        cols.append(blk.T)
    gcol = jnp.concatenate(cols, axis=0)
    for c in range(tm // COMB_CH):
        rows = slice(c * COMB_CH, (c + 1) * COMB_CH)
        gk = [gcol[rows, k:k + 1] for k in range(TOP_K)]
        sq = jnp.zeros((COMB_CH, LANES), F32)
        for j in range(ys):
            a = x1_ref[rows, j * LANES:(j + 1) * LANES]
            for k in range(TOP_K):
                a = a + gbuf[k, j, rows, :] * gk[k]
            sq = sq + a * a
            o_ref[rows, j * LANES:(j + 1) * LANES] = a
        ms = jnp.sum(sq, axis=-1, keepdims=True) * (1.0 / o_ref.shape[1])
        o_ref[rows, :] = o_ref[rows, :] * lax.rsqrt(ms + RMS_EPS) * gf_ref[...]


def _combine(dest, gates, x1, gf, y_buf):
    n, d = x1.shape
    tm = COMB_TM
    return pl.pallas_call(
        _combine_kernel,
        grid=(n // tm,),
        in_specs=[pl.BlockSpec((tm // TOK_PER_ROW, LANES), lambda i: (i, 0), memory_space=pltpu.SMEM),
                  pl.BlockSpec((TOP_K, tm), lambda i: (0, i)),
                  pl.BlockSpec((tm, d), lambda i: (i, 0)),
                  pl.BlockSpec((1, d), lambda i: (0, 0)),
                  pl.BlockSpec(memory_space=pl.ANY)],
        out_specs=pl.BlockSpec((tm, d), lambda i: (i, 0)),
        out_shape=jax.ShapeDtypeStruct((n, d), F32),
        scratch_shapes=[pltpu.VMEM((TOP_K, d // LANES, tm, LANES), F32),
                        pltpu.SemaphoreType.DMA(())],
        compiler_params=_cparams("arbitrary"),
        name="combine",
    )(dest, gates, x1, gf.reshape(1, d), y_buf)


def kernel(x, positions, norm1_g, w_in, lambda_q1, lambda_k1, lambda_q2, lambda_k2, diff_norm_g, ret_norm_g, w_o, norm2_g, w_router, b_router, w_moe_in, b_moe_in, w_moe_out, b_moe_out, norm_f_g):
    batch, seq, d = x.shape
    n = batch * seq
    assert norm1_g.shape[0] == 1, "single-layer block"
    l = 0
    p_rows = n * TOP_K + N_EXPERTS * MOE_BM
    nb = p_rows // MOE_BM
    nb_pad = -(-nb // LANES) * LANES

    cos, sin = _rope_tables(positions)
    x2d = x.reshape(n, d)
    lam_init = 0.8 - 0.6 * math.exp(-0.3 * l)
    proj = _in_proj(x2d, norm1_g[l], w_in[l].astype(BF16), cos, sin)
    d_out = _diff_attention(proj, lambda_q1[l], lambda_k1[l], lambda_q2[l], lambda_k2[l],
                            diff_norm_g[l], batch, seq, lam_init)
    r_out = _retention(proj, ret_norm_g[l], batch, seq)
    x1, h2p, top_idx, gates = _out_proj(d_out, r_out, w_o[l].astype(BF16), x2d, norm2_g[l],
                                        w_router[l].T, b_router[l])
    dest, blk_expert, ends = _plan(top_idx, nb_pad)
    ends = ends.reshape(-1)
    dest = dest.T.reshape(n // TOK_PER_ROW, LANES)
    xs, ys = d // 2 // LANES, d // LANES
    x_buf = _dispatch(ends, dest, h2p.reshape(n, xs, LANES), p_rows)
    b1 = _pair_group(b_moe_in[l])
    b2 = b_moe_out[l].reshape(N_EXPERTS, 1, -1)
    y_buf = _experts(blk_expert.reshape(-1), ends, x_buf.reshape(p_rows * xs, LANES),
                     w_moe_in[l], b1, w_moe_out[l], b2)
    out = _combine(dest, gates, x1, norm_f_g, y_buf.reshape(p_rows, ys, 1, LANES))
    return out.reshape(batch, seq, d)
```

```python
import functools
import math

import numpy as np
import jax
import jax.numpy as jnp
from jax import lax
from jax.experimental import pallas as pl
from jax.experimental.pallas import tpu as pltpu

F32 = jnp.float32
BF16 = jnp.bfloat16
I32 = jnp.int32
U32 = jnp.uint32

N_DIFF_HEADS = 4
DIFF_QK_DIM = 64
DIFF_V_DIM = 128
DIFF_WIDTH = N_DIFF_HEADS * DIFF_V_DIM
N_RET_HEADS = 4
RET_QK_DIM = 64
RET_V_DIM = 128
RET_WIDTH = N_RET_HEADS * RET_V_DIM
ROPE_THETA = 10000.0
RMS_EPS = 1e-5
N_EXPERTS = 32
TOP_K = 4
SWIGLU_LIMIT = 7.0
SWIGLU_ALPHA = 1.702
TOK_PER_ROW = 128 // TOP_K
COL_DQ = 0
COL_DK = 512
COL_DV = 1024
COL_RQ = 1536
COL_RK = 1792
COL_RV = 2048
COL_RG = 2560
PROJ_WIDTH = 3072
LOG_DECAY = tuple(math.log(1.0 - 2.0 ** (-5.0 - h)) for h in range(N_RET_HEADS))
LOG2E = 1.4426950408889634

LANES = 128
SUBLANES = 8
MXU_DIM = 256
VMEM_LIMIT = 56 * 1024 * 1024

ROPE_ROWS = 1024
PROJ_TM = 512
ATT_TQ = 1024
ATT_TK = 1024
RET_T = 512
RET_C = 256
OUT_TM = 512
PLAN_TC = 512
DISP_TM = 1024
MOE_BM = 512
COMB_TM = 512
COMB_CH = 64

NEG_BIG = -1e30


def _cparams(*sem):
    return pltpu.CompilerParams(dimension_semantics=sem, vmem_limit_bytes=VMEM_LIMIT)


def _rope_table_kernel(pos_ref, invf_ref, cos_ref, sin_ref):
    invf = invf_ref[...]
    lane = lax.broadcasted_iota(I32, (LANES, LANES), 1)
    first_half = (lane & 32) == 0
    nf = DIFF_QK_DIM // 2
    groups = LANES // nf
    for r0 in range(0, ROPE_ROWS // LANES, groups):
        stacked = jnp.concatenate(
            [jnp.broadcast_to(pos_ref[r0 + g:r0 + g + 1, :].astype(F32), (nf, LANES)) for g in range(groups)],
            axis=0)
        ang = stacked.T * invf
        c4 = jnp.cos(ang)
        s4 = jnp.sin(ang)
        for g in range(groups):
            mine = (lane // nf) == g

            def spread(t):
                z = jnp.where(mine, t, 0.0)
                return z + pltpu.roll(z, nf, 1) + pltpu.roll(z, 2 * nf, 1) + pltpu.roll(z, 3 * nf, 1)

            rows = slice((r0 + g) * LANES, (r0 + g + 1) * LANES)
            s = spread(s4)
            cos_ref[rows, :] = spread(c4)
            sin_ref[rows, :] = jnp.where(first_half, -s, s)


def _rope_tables(positions):
    n = positions.size
    pos2d = positions.reshape(n // LANES, LANES)
    d = DIFF_QK_DIM
    inv_freq = 1.0 / (ROPE_THETA ** (jnp.arange(0, d, 2, dtype=F32) / d))
    invf = jnp.tile(inv_freq, LANES // (d // 2)).reshape(1, LANES)
    rows = ROPE_ROWS // LANES
    return pl.pallas_call(
        _rope_table_kernel,
        grid=(n // ROPE_ROWS,),
        in_specs=[pl.BlockSpec((rows, LANES), lambda i: (i, 0)),
                  pl.BlockSpec((1, LANES), lambda i: (0, 0))],
        out_specs=[pl.BlockSpec((ROPE_ROWS, LANES), lambda i: (i, 0)),
                   pl.BlockSpec((ROPE_ROWS, LANES), lambda i: (i, 0))],
        out_shape=[jax.ShapeDtypeStruct((n, LANES), F32)] * 2,
        compiler_params=_cparams("arbitrary"),
        name="rope_table",
    )(pos2d, invf)


def _inproj_kernel(x_ref, g_ref, w_ref, cos_ref, sin_ref, o_ref):
    x = x_ref[...]
    ms = jnp.mean(x * x, axis=-1, keepdims=True)
    h = (x * lax.rsqrt(ms + RMS_EPS) * g_ref[...]).astype(BF16)
    cos = cos_ref[...]
    sin = sin_ref[...]
    lane = lax.broadcasted_iota(I32, cos.shape, 1)
    first_half = (lane & 32) == 0

    def rope(t):
        rot = jnp.where(first_half, pltpu.roll(t, 96, 1), pltpu.roll(t, 32, 1))
        return t * cos + rot * sin

    q_scale = DIFF_QK_DIM ** -0.5 * LOG2E
    k_scale = RET_QK_DIM ** -0.5
    for c in range(PROJ_WIDTH // MXU_DIM):
        p = jnp.dot(h, w_ref[:, c * MXU_DIM:(c + 1) * MXU_DIM], preferred_element_type=F32)
        for half in range(MXU_DIM // LANES):
            col = c * MXU_DIM + half * LANES
            t = p[:, half * LANES:(half + 1) * LANES]
            if col < COL_DK:
                t = rope(t) * q_scale
            elif col < COL_DV:
                t = rope(t)
            elif col < COL_RQ:
                pass
            elif col < COL_RK:
                t = rope(t)
            elif col < COL_RV:
                t = rope(t) * k_scale
            elif col < COL_RG:
                pass
            else:
                t = t * jax.nn.sigmoid(t)
            o_ref[:, col:col + LANES] = t.astype(BF16)


def _in_proj(x2d, g1, w_in_b, cos, sin):
    n, d = x2d.shape
    tm = PROJ_TM
    return pl.pallas_call(
        _inproj_kernel,
        grid=(n // tm,),
        in_specs=[pl.BlockSpec((tm, d), lambda i: (i, 0)),
                  pl.BlockSpec((1, d), lambda i: (0, 0)),
                  pl.BlockSpec((d, PROJ_WIDTH), lambda i: (0, 0)),
                  pl.BlockSpec((tm, LANES), lambda i: (i, 0)),
                  pl.BlockSpec((tm, LANES), lambda i: (i, 0))],
        out_specs=pl.BlockSpec((tm, PROJ_WIDTH), lambda i: (i, 0)),
        out_shape=jax.ShapeDtypeStruct((n, PROJ_WIDTH), BF16),
        compiler_params=_cparams("arbitrary"),
        name="in_proj",
    )(x2d, g1.reshape(1, d), w_in_b, cos, sin)


def _diff_attn_kernel(lq1_ref, lk1_ref, lq2_ref, lk2_ref, g_ref, q_ref, k_ref, v_ref, o_ref,
                      qq_sc, m_sc, l_sc, acc_sc, *, lam_init):
    tq, tk = ATT_TQ, ATT_TK
    i = pl.program_id(2)
    q = q_ref[...]
    lane = lax.broadcasted_iota(I32, q.shape, 1)
    zero = jnp.zeros_like(q)
    qq_sc[:tq, :] = jnp.where(lane < DIFF_QK_DIM, q, zero)
    qq_sc[tq:, :] = jnp.where(lane >= DIFF_QK_DIM, q, zero)
    m_sc[...] = jnp.full(m_sc.shape, NEG_BIG, F32)
    l_sc[...] = jnp.zeros(l_sc.shape, F32)
    acc_sc[...] = jnp.zeros(acc_sc.shape, F32)

    def step(start, width, masked):
        k = k_ref[pl.ds(start, width), :]
        v = v_ref[pl.ds(start, width), :]
        s = lax.dot_general(qq_sc[...], k, (((1,), (1,)), ((), ())), preferred_element_type=F32)
        if masked:
            row = lax.broadcasted_iota(I32, s.shape, 0) & (tq - 1)
            col = lax.broadcasted_iota(I32, s.shape, 1)
            s = jnp.where(col <= row, s, NEG_BIG)
        m_prev = m_sc[...]
        m_next = jnp.maximum(m_prev, jnp.max(s, axis=1, keepdims=True))
        alpha = jnp.exp2(m_prev - m_next)
        p = jnp.exp2(s - jnp.concatenate([m_next] * (width // LANES), axis=1))
        psum = p[:, :LANES]
        for c in range(1, width // LANES):
            psum = psum + p[:, c * LANES:(c + 1) * LANES]
        l_sc[...] = alpha * l_sc[...] + psum
        acc_sc[...] = alpha * acc_sc[...] + jnp.dot(p.astype(BF16), v, preferred_element_type=F32)
        m_sc[...] = m_next

    def body(j, carry):
        step(pl.multiple_of(j * tk, tk), tk, False)
        return carry

    lax.fori_loop(0, i, body, 0)
    step(pl.multiple_of(i * tk, tk), tk, True)

    lam = (jnp.exp(jnp.sum(lq1_ref[...] * lk1_ref[...], axis=-1, keepdims=True))
           - jnp.exp(jnp.sum(lq2_ref[...] * lk2_ref[...], axis=-1, keepdims=True))
           + lam_init)
    o = acc_sc[...] / jnp.sum(l_sc[...], axis=1, keepdims=True)
    d = o[:tq, :] - lam * o[tq:, :]
    ms = jnp.mean(d * d, axis=-1, keepdims=True)
    out = d * lax.rsqrt(ms + RMS_EPS) * g_ref[...] * (1.0 - lam_init)
    o_ref[...] = out.astype(BF16)


def _diff_attention(proj, lq1, lk1, lq2, lk2, g, batch, seq, lam_init):
    n = batch * seq
    tq = ATT_TQ
    assert ATT_TQ == ATT_TK and seq % tq == 0 and DIFF_V_DIM == LANES
    nq = seq // tq
    qcol = COL_DQ // LANES
    kcol = COL_DK // LANES
    vcol = COL_DV // LANES
    vec = lambda b, h, i: (0, 0)
    return pl.pallas_call(
        functools.partial(_diff_attn_kernel, lam_init=lam_init),
        grid=(batch, N_DIFF_HEADS, nq),
        in_specs=[pl.BlockSpec((1, DIFF_QK_DIM), vec)] * 4 + [
            pl.BlockSpec((1, DIFF_V_DIM), vec),
            pl.BlockSpec((tq, LANES), lambda b, h, i: (b * nq + i, qcol + h)),
            pl.BlockSpec((seq, LANES), lambda b, h, i: (b, kcol + h)),
            pl.BlockSpec((seq, LANES), lambda b, h, i: (b, vcol + h))],
        out_specs=pl.BlockSpec((tq, LANES), lambda b, h, i: (b * nq + i, h)),
        out_shape=jax.ShapeDtypeStruct((n, DIFF_WIDTH), BF16),
        scratch_shapes=[pltpu.VMEM((2 * tq, LANES), BF16),
                        pltpu.VMEM((2 * tq, LANES), F32),
                        pltpu.VMEM((2 * tq, LANES), F32),
                        pltpu.VMEM((2 * tq, DIFF_V_DIM), F32)],
        compiler_params=_cparams("arbitrary", "arbitrary", "arbitrary"),
        name="diff_attn",
    )(lq1.reshape(1, -1), lk1.reshape(1, -1), lq2.reshape(1, -1), lk2.reshape(1, -1),
      g.reshape(1, -1), proj, proj, proj)


def _retention_kernel(q_ref, k_ref, v_ref, gate_ref, g_ref, o_ref, state_sc):
    c_len = RET_C

    @pl.when(pl.program_id(1) == 0)
    def _():
        state_sc[...] = jnp.zeros(state_sc.shape, F32)

    ii = lax.broadcasted_iota(I32, (c_len, c_len), 0)
    jj = lax.broadcasted_iota(I32, (c_len, c_len), 1)
    rel = (ii - jj).astype(F32)
    lane = lax.broadcasted_iota(I32, (c_len, LANES), 1)
    pos = lax.broadcasted_iota(I32, (c_len, LANES), 0).astype(F32)
    srow = lax.broadcasted_iota(I32, (LANES, LANES), 0)
    for pair in range(N_RET_HEADS // 2):
        ld = (LOG_DECAY[2 * pair], LOG_DECAY[2 * pair + 1])
        ld_lane = jnp.where(lane < RET_QK_DIM, ld[0], ld[1])
        q_decay = jnp.exp(ld_lane * (pos + 1.0))
        k_decay = jnp.exp(ld_lane * (c_len - 1.0 - pos))
        chunk_decay = jnp.where(srow < RET_QK_DIM, math.exp(ld[0] * c_len), math.exp(ld[1] * c_len))
        intra = [jnp.where(rel >= 0, jnp.exp(l * jnp.maximum(rel, 0.0)), 0.0) for l in ld]
        in_head = (lane < RET_QK_DIM, lane >= RET_QK_DIM)
        for c in range(RET_T // c_len):
            rows = slice(c * c_len, (c + 1) * c_len)
            qb = q_ref[rows, pair * LANES:(pair + 1) * LANES]
            kb = k_ref[rows, pair * LANES:(pair + 1) * LANES]
            q = qb.astype(F32)
            state = state_sc[pair]
            state_b = state.astype(BF16)
            kd_t = (kb.astype(F32) * k_decay).T.astype(BF16)
            new_kv = []
            for hh in range(2):
                h = 2 * pair + hh
                qm = jnp.where(in_head[hh], q, 0.0)
                s = lax.dot_general(qm.astype(BF16), kb, (((1,), (1,)), ((), ())),
                                    preferred_element_type=F32) * intra[hh]
                v = v_ref[rows, h * RET_V_DIM:(h + 1) * RET_V_DIM]
                y = (jnp.dot(s.astype(BF16), v, preferred_element_type=F32)
                     + jnp.dot((qm * q_decay).astype(BF16), state_b, preferred_element_type=F32))
                new_kv.append(jnp.dot(kd_t, v, preferred_element_type=F32))
                ms = jnp.mean(y * y, axis=-1, keepdims=True)
                yn = y * lax.rsqrt(ms + RMS_EPS) * g_ref[h:h + 1, :]
                gate = gate_ref[rows, h * RET_V_DIM:(h + 1) * RET_V_DIM].astype(F32)
                o_ref[rows, h * RET_V_DIM:(h + 1) * RET_V_DIM] = (yn * gate).astype(BF16)
            state_sc[pair] = chunk_decay * state + jnp.where(srow < RET_QK_DIM, new_kv[0], new_kv[1])


def _retention(proj, g, batch, seq):
    n = batch * seq
    t = RET_T
    nt = seq // t
    qk_w = N_RET_HEADS * RET_QK_DIM
    return pl.pallas_call(
        _retention_kernel,
        grid=(batch, nt),
        in_specs=[pl.BlockSpec((t, qk_w), lambda b, i: (b * nt + i, COL_RQ // qk_w)),
                  pl.BlockSpec((t, qk_w), lambda b, i: (b * nt + i, COL_RK // qk_w)),
                  pl.BlockSpec((t, RET_WIDTH), lambda b, i: (b * nt + i, COL_RV // RET_WIDTH)),
                  pl.BlockSpec((t, RET_WIDTH), lambda b, i: (b * nt + i, COL_RG // RET_WIDTH)),
                  pl.BlockSpec((N_RET_HEADS, RET_V_DIM), lambda b, i: (0, 0))],
        out_specs=pl.BlockSpec((t, RET_WIDTH), lambda b, i: (b * nt + i, 0)),
        out_shape=jax.ShapeDtypeStruct((n, RET_WIDTH), BF16),
        scratch_shapes=[pltpu.VMEM((N_RET_HEADS // 2, LANES, RET_V_DIM), F32)],
        compiler_params=_cparams("arbitrary", "arbitrary"),
        name="retention",
    )(proj, proj, proj, proj, g)


def _outproj_kernel(d_ref, r_ref, wo_ref, x_ref, g2_ref, wr_ref, br_ref,
                    x1_ref, h2p_ref, idx_ref, gate_ref):
    acc = (jnp.dot(d_ref[...], wo_ref[:DIFF_WIDTH, :], preferred_element_type=F32)
           + jnp.dot(r_ref[...], wo_ref[DIFF_WIDTH:, :], preferred_element_type=F32))
    x1 = x_ref[...] + acc
    x1_ref[...] = x1
    ms = jnp.mean(x1 * x1, axis=-1, keepdims=True)
    h2 = x1 * lax.rsqrt(ms + RMS_EPS) * g2_ref[...]
    half = h2.shape[1] // 2
    packed = pltpu.pack_elementwise([h2[:, :half], h2[:, half:]], packed_dtype=BF16)
    slabs = half // LANES
    for j in range(slabs):
        h2p_ref[pl.ds(j, packed.shape[0], stride=slabs), :] = packed[:, j * LANES:(j + 1) * LANES]

    nt = (((1,), (1,)), ((), ()))
    wr = wr_ref[...]
    wr_hi = wr.astype(BF16)
    wr_lo = (wr - wr_hi.astype(F32)).astype(BF16)
    h2_hi = h2.astype(BF16)
    h2_lo = (h2 - h2_hi.astype(F32)).astype(BF16)
    logits = (lax.dot_general(wr_hi, h2_hi, nt, preferred_element_type=F32)
              + lax.dot_general(wr_hi, h2_lo, nt, preferred_element_type=F32)
              + lax.dot_general(wr_lo, h2_hi, nt, preferred_element_type=F32)
              + br_ref[...])
    e_iota = lax.broadcasted_iota(I32, logits.shape, 0)
    vals = []
    for r in range(TOP_K):
        m = jnp.max(logits, axis=0, keepdims=True)
        ix = jnp.min(jnp.where(logits == m, e_iota, N_EXPERTS), axis=0, keepdims=True)
        vals.append(m)
        idx_ref[r:r + 1, :] = ix
        logits = jnp.where(e_iota == ix, -jnp.inf, logits)
    ex = [jnp.exp(v - vals[0]) for v in vals]
    den = ex[0] + ex[1] + ex[2] + ex[3]
    for r in range(TOP_K):
        gate_ref[r:r + 1, :] = ex[r] / den


def _out_proj(d_out, r_out, w_o_b, x2d, g2, w_router_t, b_router):
    n, d = x2d.shape
    tm = OUT_TM
    const = lambda i: (0, 0)
    return pl.pallas_call(
        _outproj_kernel,
        grid=(n // tm,),
        in_specs=[pl.BlockSpec((tm, DIFF_WIDTH), lambda i: (i, 0)),
                  pl.BlockSpec((tm, RET_WIDTH), lambda i: (i, 0)),
                  pl.BlockSpec((DIFF_WIDTH + RET_WIDTH, d), const),
                  pl.BlockSpec((tm, d), lambda i: (i, 0)),
                  pl.BlockSpec((1, d), const),
                  pl.BlockSpec((N_EXPERTS, d), const),
                  pl.BlockSpec((N_EXPERTS, 1), const)],
        out_specs=[pl.BlockSpec((tm, d), lambda i: (i, 0)),
                   pl.BlockSpec((tm * (d // 2 // LANES), LANES), lambda i: (i, 0)),
                   pl.BlockSpec((TOP_K, tm), lambda i: (0, i)),
                   pl.BlockSpec((TOP_K, tm), lambda i: (0, i))],
        out_shape=[jax.ShapeDtypeStruct((n, d), F32),
                   jax.ShapeDtypeStruct((n * (d // 2 // LANES), LANES), U32),
                   jax.ShapeDtypeStruct((TOP_K, n), I32),
                   jax.ShapeDtypeStruct((TOP_K, n), F32)],
        compiler_params=_cparams("arbitrary"),
        name="out_proj",
    )(d_out, r_out, w_o_b, x2d, g2.reshape(1, d), w_router_t, b_router.reshape(N_EXPERTS, 1))


def _plan_kernel(idx_ref, dest_ref, blk_ref, ends_ref, cnt_sc, base_sc, tri_sc, *, nb_pad):
    ph = pl.program_id(0)
    c = pl.program_id(1)
    tc = PLAN_TC
    e_iota = lax.broadcasted_iota(I32, (N_EXPERTS, tc), 0)

    @pl.when((ph == 0) & (c == 0))
    def _():
        cnt_sc[...] = jnp.zeros(cnt_sc.shape, F32)
        s = lax.broadcasted_iota(I32, (tc, tc), 0)
        t = lax.broadcasted_iota(I32, (tc, tc), 1)
        tri_sc[...] = jnp.where(s < t, 1.0, 0.0).astype(BF16)

    @pl.when(ph == 0)
    def _():
        tot = jnp.zeros((N_EXPERTS, 1), F32)
        for k in range(TOP_K):
            oh = idx_ref[k:k + 1, :] == e_iota
            tot = tot + jnp.sum(jnp.where(oh, 1.0, 0.0), axis=1, keepdims=True)
        cnt_sc[...] = cnt_sc[...] + tot

    @pl.when((ph == 1) & (c == 0))
    def _():
        cnt = cnt_sc[...]
        nblk = jnp.floor((cnt + (MOE_BM - 1.0)) * (1.0 / MOE_BM))
        ei = lax.broadcasted_iota(I32, (N_EXPERTS, LANES), 0)
        li = lax.broadcasted_iota(I32, (N_EXPERTS, LANES), 1)
        nblk_row = jnp.sum(jnp.where(ei == li, nblk, 0.0), axis=0, keepdims=True)
        start = jnp.sum(jnp.where(li < ei, nblk_row, 0.0), axis=1, keepdims=True)
        base_sc[...] = start * MOE_BM
        end = start + nblk
        bi = lax.broadcasted_iota(I32, (N_EXPERTS, nb_pad), 1).astype(F32)
        be = jnp.sum(jnp.where(end <= bi, 1.0, 0.0), axis=0, keepdims=True)
        blk_ref[...] = jnp.minimum(be, N_EXPERTS - 1.0).astype(I32)
        ends_ref[...] = jnp.sum(jnp.where(ei == li, end, 0.0), axis=0, keepdims=True).astype(I32)

    @pl.when(ph == 1)
    def _():
        base = base_sc[...]
        for k in range(TOP_K):
            oh = idx_ref[k:k + 1, :] == e_iota
            ohb = jnp.where(oh, 1.0, 0.0).astype(BF16)
            before = jnp.dot(ohb, tri_sc[...], preferred_element_type=F32)
            rank = jnp.sum(jnp.where(oh, before + base, 0.0), axis=0, keepdims=True)
            dest_ref[k:k + 1, :] = rank.astype(I32)
            base = base + jnp.sum(jnp.where(oh, 1.0, 0.0), axis=1, keepdims=True)
        base_sc[...] = base


def _plan(top_idx, nb_pad):
    n = top_idx.shape[1]
    tc = PLAN_TC
    return pl.pallas_call(
        functools.partial(_plan_kernel, nb_pad=nb_pad),
        grid=(2, n // tc),
        in_specs=[pl.BlockSpec((TOP_K, tc), lambda ph, c: (0, c))],
        out_specs=[pl.BlockSpec((TOP_K, tc), lambda ph, c: (0, c * ph)),
                   pl.BlockSpec((1, nb_pad), lambda ph, c: (0, 0)),
                   pl.BlockSpec((1, LANES), lambda ph, c: (0, 0))],
        out_shape=[jax.ShapeDtypeStruct((TOP_K, n), I32),
                   jax.ShapeDtypeStruct((1, nb_pad), I32),
                   jax.ShapeDtypeStruct((1, LANES), I32)],
        scratch_shapes=[pltpu.VMEM((N_EXPERTS, 1), F32),
                        pltpu.VMEM((N_EXPERTS, 1), F32),
                        pltpu.VMEM((tc, tc), BF16)],
        compiler_params=_cparams("arbitrary", "arbitrary"),
        name="plan",
    )(top_idx)


def _dispatch_kernel(ends_ref, dest_ref, h_ref, xbuf_ref, zbuf, sem, zsem, *, nb):
    tm = DISP_TM
    bm = MOE_BM

    @pl.when(pl.program_id(0) == 0)
    def _():
        zbuf[...] = jnp.zeros(zbuf.shape, zbuf.dtype)

        def zero_block(blk):
            return pltpu.make_async_copy(zbuf, xbuf_ref.at[pl.ds(pl.multiple_of(blk * bm, bm), bm)], zsem)

        def per_block(fn):
            for e in range(N_EXPERTS):
                end = ends_ref[e]
                first = ends_ref[e - 1] if e else 0

                @pl.when(end > first)
                def _():
                    fn(zero_block(end - 1))

            def tail(blk, carry):
                fn(zero_block(blk))
                return carry

            lax.fori_loop(ends_ref[N_EXPERTS - 1], nb, tail, 0)

        per_block(lambda cp: cp.start())
        per_block(lambda cp: cp.wait())

    def issue(r, carry):
        for u in range(TOK_PER_ROW):
            for k in range(TOP_K):
                src = h_ref.at[r * TOK_PER_ROW + u]
                dst = xbuf_ref.at[dest_ref[r, u * TOP_K + k]]
                pltpu.make_async_copy(src, dst, sem).start(priority=k % 2)
        return carry

    lax.fori_loop(0, tm // TOK_PER_ROW, issue, 0)
    for k in range(TOP_K):
        pltpu.make_async_copy(xbuf_ref.at[pl.ds(0, tm)], xbuf_ref.at[pl.ds(0, tm)], sem).wait()


def _dispatch(ends, dest, h2p, p_rows):
    n, s, w = h2p.shape
    tm = DISP_TM
    grid_spec = pltpu.PrefetchScalarGridSpec(
        num_scalar_prefetch=1,
        grid=(n // tm,),
        in_specs=[pl.BlockSpec((tm // TOK_PER_ROW, LANES), lambda i, ends: (i, 0), memory_space=pltpu.SMEM),
                  pl.BlockSpec((tm, s, w), lambda i, ends: (i, 0, 0))],
        out_specs=pl.BlockSpec(memory_space=pl.ANY),
        scratch_shapes=[pltpu.VMEM((MOE_BM, s, w), h2p.dtype),
                        pltpu.SemaphoreType.DMA(()),
                        pltpu.SemaphoreType.DMA(())],
    )
    return pl.pallas_call(
        functools.partial(_dispatch_kernel, nb=p_rows // MOE_BM),
        grid_spec=grid_spec,
        out_shape=jax.ShapeDtypeStruct((p_rows, s, w), h2p.dtype),
        compiler_params=pltpu.CompilerParams(dimension_semantics=("arbitrary",),
                                             vmem_limit_bytes=VMEM_LIMIT,
                                             has_side_effects=True),
        name="dispatch",
    )(ends, dest, h2p)


def _pair_perm():
    a = lax.broadcasted_iota(I32, (MXU_DIM, MXU_DIM), 0)
    b = lax.broadcasted_iota(I32, (MXU_DIM, MXU_DIM), 1)
    src = jnp.where(b < LANES, 2 * b, 2 * (b - LANES) + 1)
    return jnp.where(a == src, 1.0, 0.0).astype(BF16)


def _pair_group(b):
    e, f2 = b.shape
    return b.reshape(e, f2 // MXU_DIM, LANES, 2).transpose(0, 1, 3, 2).reshape(e, 1, f2)


def _expert_kernel(blk_ref, ends_ref, x_ref, w1_hbm, b1_ref, w2_hbm, b2_ref, y_ref,
                   w1f, w2f, w1_ref, w2_ref, wsem, slot_ref):
    b = pl.program_id(0)
    n_used = ends_ref[N_EXPERTS - 1]

    @pl.when(b >= n_used)
    def _():
        zero = jnp.zeros(y_ref.shape, F32)
        y_ref[...] = pltpu.pack_elementwise([zero, zero], packed_dtype=BF16)

    def fetch(expert, slot):
        return (pltpu.make_async_copy(w1_hbm.at[expert], w1f.at[slot], wsem.at[slot]),
                pltpu.make_async_copy(w2_hbm.at[expert], w2f.at[slot], wsem.at[slot]))

    @pl.when(b == 0)
    def _():
        slot_ref[0] = 0
        for cp in fetch(blk_ref[0], 0):
            cp.start()

    new_expert = (b == 0) | (blk_ref[b] != blk_ref[jnp.maximum(b - 1, 0)])

    @pl.when((b < n_used) & new_expert)
    def _():
        slot = slot_ref[0]
        expert = blk_ref[b]
        for cp in fetch(expert, slot):
            cp.wait()
        next_first = ends_ref[expert]

        @pl.when(next_first < n_used)
        def _():
            for cp in fetch(blk_ref[next_first], 1 - slot):
                cp.start()

        perm = _pair_perm()
        for c in range(w1f.shape[2] // MXU_DIM):
            cols = slice(c * MXU_DIM, (c + 1) * MXU_DIM)
            blk = w1f[slot, :, cols].astype(BF16)
            w1_ref[0, :, cols] = jnp.dot(blk, perm, preferred_element_type=F32).astype(BF16)
        w2_ref[0] = w2f[slot].astype(BF16)
        slot_ref[0] = 1 - slot

    @pl.when(b < n_used)
    def _():
        ns = w2_ref.shape[2] // 2 // LANES
        bm = x_ref.shape[0] // ns
        slabs = [x_ref[pl.ds(j, bm, stride=ns), :] for j in range(ns)]
        lo = [pltpu.unpack_elementwise(w, index=0, packed_dtype=BF16, unpacked_dtype=F32).astype(BF16)
              for w in slabs]
        hi = [pltpu.unpack_elementwise(w, index=1, packed_dtype=BF16, unpacked_dtype=F32).astype(BF16)
              for w in slabs]
        x = jnp.concatenate(lo + hi, axis=1)
        acc = jnp.zeros((x.shape[0], w2_ref.shape[2]), F32)
        grp = 2 * MXU_DIM
        for c in range(w1_ref.shape[2] // grp):
            cols = slice(c * grp, (c + 1) * grp)
            h = jnp.dot(x, w1_ref[0, :, cols], preferred_element_type=F32) + b1_ref[0, :, cols]
            glu = jnp.concatenate([h[:, 0:LANES], h[:, 2 * LANES:3 * LANES]], axis=1)
            lin = jnp.concatenate([h[:, LANES:2 * LANES], h[:, 3 * LANES:4 * LANES]], axis=1)
            glu = jnp.minimum(glu, SWIGLU_LIMIT)
            lin = jnp.clip(lin, -SWIGLU_LIMIT, SWIGLU_LIMIT)
            act = glu * jax.nn.sigmoid(SWIGLU_ALPHA * glu) * (lin + 1.0)
            acc = acc + jnp.dot(act.astype(BF16), w2_ref[0, c * MXU_DIM:(c + 1) * MXU_DIM, :],
                                preferred_element_type=F32)
        y = acc + b2_ref[0]
        half = y.shape[1] // 2
        packed = pltpu.pack_elementwise([y[:, :half], y[:, half:]], packed_dtype=BF16)
        for j in range(ns):
            y_ref[pl.ds(j, bm, stride=ns), :] = packed[:, j * LANES:(j + 1) * LANES]


def _experts(blk_expert, ends, x_buf, w1, b1, w2, b2):
    e, d, f2 = w1.shape
    xs = d // 2 // LANES
    ys = xs
    p_rows = x_buf.shape[0] // xs
    f = f2 // 2
    bm = MOE_BM
    nb = p_rows // bm

    def row_blk(b, ends):
        return jnp.minimum(b, ends[N_EXPERTS - 1] - 1)

    grid_spec = pltpu.PrefetchScalarGridSpec(
        num_scalar_prefetch=2,
        grid=(nb,),
        in_specs=[pl.BlockSpec((bm * xs, LANES), lambda b, blk, ends: (row_blk(b, ends), 0)),
                  pl.BlockSpec(memory_space=pl.ANY),
                  pl.BlockSpec((1, 1, f2), lambda b, blk, ends: (blk[row_blk(b, ends)], 0, 0)),
                  pl.BlockSpec(memory_space=pl.ANY),
                  pl.BlockSpec((1, 1, d), lambda b, blk, ends: (blk[row_blk(b, ends)], 0, 0))],
        out_specs=pl.BlockSpec((bm * ys, LANES), lambda b, blk, ends: (b, 0)),
        scratch_shapes=[pltpu.VMEM((2, d, f2), F32),
                        pltpu.VMEM((2, f, d), F32),
                        pltpu.VMEM((1, d, f2), BF16),
                        pltpu.VMEM((1, f, d), BF16),
                        pltpu.SemaphoreType.DMA((2,)),
                        pltpu.SMEM((1,), I32)],
    )
    return pl.pallas_call(
        _expert_kernel,
        grid_spec=grid_spec,
        out_shape=jax.ShapeDtypeStruct((p_rows * ys, LANES), x_buf.dtype),
        compiler_params=_cparams("arbitrary"),
        name="experts",
    )(blk_expert, ends, x_buf, w1, b1, w2, b2)


def _combine_kernel(dest_ref, gate_ref, x1_ref, gf_ref, ybuf_ref, o_ref, gbuf, sem):
    tm = COMB_TM

    ys = ybuf_ref.shape[1]

    def issue(r, carry):
        for u in range(TOK_PER_ROW):
            for k in range(TOP_K):
                src = ybuf_ref.at[dest_ref[r, u * TOP_K + k]]
                dst = gbuf.at[k, :, pl.ds(r * TOK_PER_ROW + u, 1), :]
                pltpu.make_async_copy(src, dst, sem).start(priority=k % 2)
        return carry

    lax.fori_loop(0, tm // TOK_PER_ROW, issue, 0)
    for k in range(TOP_K):
        pltpu.make_async_copy(ybuf_ref.at[pl.ds(0, tm)], ybuf_ref.at[pl.ds(0, tm)], sem).wait()

    gates = gate_ref[...]
    pad = jnp.zeros((LANES - TOP_K, LANES), F32)
    cols = []
    for c in range(tm // LANES):
        blk = jnp.concatenate([gates[:, c * LANES:(c + 1) * LANES], pad], axis=0)
        cols.append(blk.T)
    gcol = jnp.concatenate(cols, axis=0)
    for c in range(tm // COMB_CH):
        rows = slice(c * COMB_CH, (c + 1) * COMB_CH)
        gk = [gcol[rows, k:k + 1] for k in range(TOP_K)]
        sq = jnp.zeros((COMB_CH, LANES), F32)
        for j in range(ys):
            words = [gbuf[k, j, rows, :] for k in range(TOP_K)]
            for part in range(2):
                col = (part * ys + j) * LANES
                a = x1_ref[rows, col:col + LANES]
                for k in range(TOP_K):
                    yk = pltpu.unpack_elementwise(words[k], index=part, packed_dtype=BF16, unpacked_dtype=F32)
                    a = a + yk * gk[k]
                sq = sq + a * a
                o_ref[rows, col:col + LANES] = a
        ms = jnp.sum(sq, axis=-1, keepdims=True) * (1.0 / o_ref.shape[1])
        o_ref[rows, :] = o_ref[rows, :] * lax.rsqrt(ms + RMS_EPS) * gf_ref[...]


def _combine(dest, gates, x1, gf, y_buf):
    n, d = x1.shape
    tm = COMB_TM
    return pl.pallas_call(
        _combine_kernel,
        grid=(n // tm,),
        in_specs=[pl.BlockSpec((tm // TOK_PER_ROW, LANES), lambda i: (i, 0), memory_space=pltpu.SMEM),
                  pl.BlockSpec((TOP_K, tm), lambda i: (0, i)),
                  pl.BlockSpec((tm, d), lambda i: (i, 0)),
                  pl.BlockSpec((1, d), lambda i: (0, 0)),
                  pl.BlockSpec(memory_space=pl.ANY)],
        out_specs=pl.BlockSpec((tm, d), lambda i: (i, 0)),
        out_shape=jax.ShapeDtypeStruct((n, d), F32),
        scratch_shapes=[pltpu.VMEM((TOP_K, y_buf.shape[1], tm, LANES), y_buf.dtype),
                        pltpu.SemaphoreType.DMA(())],
        compiler_params=_cparams("arbitrary"),
        name="combine",
    )(dest, gates, x1, gf.reshape(1, d), y_buf)


def kernel(x, positions, norm1_g, w_in, lambda_q1, lambda_k1, lambda_q2, lambda_k2, diff_norm_g, ret_norm_g, w_o, norm2_g, w_router, b_router, w_moe_in, b_moe_in, w_moe_out, b_moe_out, norm_f_g):
    batch, seq, d = x.shape
    n = batch * seq
    assert norm1_g.shape[0] == 1, "single-layer block"
    l = 0
    p_rows = n * TOP_K + N_EXPERTS * MOE_BM
    nb = p_rows // MOE_BM
    nb_pad = -(-nb // LANES) * LANES

    cos, sin = _rope_tables(positions)
    x2d = x.reshape(n, d)
    lam_init = 0.8 - 0.6 * math.exp(-0.3 * l)
    proj = _in_proj(x2d, norm1_g[l], w_in[l].astype(BF16), cos, sin)
    d_out = _diff_attention(proj, lambda_q1[l], lambda_k1[l], lambda_q2[l], lambda_k2[l],
                            diff_norm_g[l], batch, seq, lam_init)
    r_out = _retention(proj, ret_norm_g[l], batch, seq)
    x1, h2p, top_idx, gates = _out_proj(d_out, r_out, w_o[l].astype(BF16), x2d, norm2_g[l],
                                        w_router[l].T, b_router[l])
    dest, blk_expert, ends = _plan(top_idx, nb_pad)
    ends = ends.reshape(-1)
    dest = dest.T.reshape(n // TOK_PER_ROW, LANES)
    xs = ys = d // 2 // LANES
    x_buf = _dispatch(ends, dest, h2p.reshape(n, xs, LANES), p_rows)
    b1 = _pair_group(b_moe_in[l])
    b2 = b_moe_out[l].reshape(N_EXPERTS, 1, -1)
    y_buf = _experts(blk_expert.reshape(-1), ends, x_buf.reshape(p_rows * xs, LANES),
                     w_moe_in[l], b1, w_moe_out[l], b2)
    out = _combine(dest, gates, x1, norm_f_g, y_buf.reshape(p_rows, ys, 1, LANES))
    return out.reshape(batch, seq, d)
```

```python
import functools
import math

import numpy as np
import jax
import jax.numpy as jnp
from jax import lax
from jax.experimental import pallas as pl
from jax.experimental.pallas import tpu as pltpu

F32 = jnp.float32
BF16 = jnp.bfloat16
I32 = jnp.int32
U32 = jnp.uint32

N_DIFF_HEADS = 4
DIFF_QK_DIM = 64
DIFF_V_DIM = 128
DIFF_WIDTH = N_DIFF_HEADS * DIFF_V_DIM
N_RET_HEADS = 4
RET_QK_DIM = 64
RET_V_DIM = 128
RET_WIDTH = N_RET_HEADS * RET_V_DIM
ROPE_THETA = 10000.0
RMS_EPS = 1e-5
N_EXPERTS = 32
TOP_K = 4
SWIGLU_LIMIT = 7.0
SWIGLU_ALPHA = 1.702
TOK_PER_ROW = 128 // TOP_K
COL_DQ = 0
COL_DK = 512
COL_DV = 1024
COL_RQ = 1536
COL_RK = 1792
COL_RV = 2048
COL_RG = 2560
PROJ_WIDTH = 3072
LOG_DECAY = tuple(math.log(1.0 - 2.0 ** (-5.0 - h)) for h in range(N_RET_HEADS))
LOG2E = 1.4426950408889634

LANES = 128
SUBLANES = 8
MXU_DIM = 256
VMEM_LIMIT = 56 * 1024 * 1024

ROPE_ROWS = 1024
PROJ_TM = 512
ATT_TQ = 1024
ATT_TK = 1024
ATT_PARTS = 2
RET_T = 512
RET_C = 256
OUT_TM = 512
PLAN_TC = 512
DISP_TM = 1024
MOE_BM = 512
COMB_TM = 512
COMB_CH = 64

NEG_BIG = -1e30


def _cparams(*sem):
    return pltpu.CompilerParams(dimension_semantics=sem, vmem_limit_bytes=VMEM_LIMIT)


def _rope_table_kernel(pos_ref, invf_ref, cos_ref, sin_ref):
    invf = invf_ref[...]
    lane = lax.broadcasted_iota(I32, (LANES, LANES), 1)
    first_half = (lane & 32) == 0
    nf = DIFF_QK_DIM // 2
    groups = LANES // nf
    for r0 in range(0, ROPE_ROWS // LANES, groups):
        stacked = jnp.concatenate(
            [jnp.broadcast_to(pos_ref[r0 + g:r0 + g + 1, :].astype(F32), (nf, LANES)) for g in range(groups)],
            axis=0)
        ang = stacked.T * invf
        c4 = jnp.cos(ang)
        s4 = jnp.sin(ang)
        for g in range(groups):
            mine = (lane // nf) == g

            def spread(t):
                z = jnp.where(mine, t, 0.0)
                return z + pltpu.roll(z, nf, 1) + pltpu.roll(z, 2 * nf, 1) + pltpu.roll(z, 3 * nf, 1)

            rows = slice((r0 + g) * LANES, (r0 + g + 1) * LANES)
            s = spread(s4)
            cos_ref[rows, :] = spread(c4)
            sin_ref[rows, :] = jnp.where(first_half, -s, s)


def _rope_tables(positions):
    n = positions.size
    pos2d = positions.reshape(n // LANES, LANES)
    d = DIFF_QK_DIM
    inv_freq = 1.0 / (ROPE_THETA ** (jnp.arange(0, d, 2, dtype=F32) / d))
    invf = jnp.tile(inv_freq, LANES // (d // 2)).reshape(1, LANES)
    rows = ROPE_ROWS // LANES
    return pl.pallas_call(
        _rope_table_kernel,
        grid=(n // ROPE_ROWS,),
        in_specs=[pl.BlockSpec((rows, LANES), lambda i: (i, 0)),
                  pl.BlockSpec((1, LANES), lambda i: (0, 0))],
        out_specs=[pl.BlockSpec((ROPE_ROWS, LANES), lambda i: (i, 0)),
                   pl.BlockSpec((ROPE_ROWS, LANES), lambda i: (i, 0))],
        out_shape=[jax.ShapeDtypeStruct((n, LANES), F32)] * 2,
        compiler_params=_cparams("arbitrary"),
        name="rope_table",
    )(pos2d, invf)


def _inproj_kernel(x_ref, g_ref, w_ref, cos_ref, sin_ref, o_ref):
    x = x_ref[...]
    ms = jnp.mean(x * x, axis=-1, keepdims=True)
    h = (x * lax.rsqrt(ms + RMS_EPS) * g_ref[...]).astype(BF16)
    cos = cos_ref[...]
    sin = sin_ref[...]
    lane = lax.broadcasted_iota(I32, cos.shape, 1)
    first_half = (lane & 32) == 0

    def rope(t):
        rot = jnp.where(first_half, pltpu.roll(t, 96, 1), pltpu.roll(t, 32, 1))
        return t * cos + rot * sin

    q_scale = DIFF_QK_DIM ** -0.5 * LOG2E
    k_scale = RET_QK_DIM ** -0.5
    for c in range(PROJ_WIDTH // MXU_DIM):
        p = jnp.dot(h, w_ref[:, c * MXU_DIM:(c + 1) * MXU_DIM], preferred_element_type=F32)
        for half in range(MXU_DIM // LANES):
            col = c * MXU_DIM + half * LANES
            t = p[:, half * LANES:(half + 1) * LANES]
            if col < COL_DK:
                t = rope(t) * q_scale
            elif col < COL_DV:
                t = rope(t)
            elif col < COL_RQ:
                pass
            elif col < COL_RK:
                t = rope(t)
            elif col < COL_RV:
                t = rope(t) * k_scale
            elif col < COL_RG:
                pass
            else:
                t = t * jax.nn.sigmoid(t)
            o_ref[:, col:col + LANES] = t.astype(BF16)


def _in_proj(x2d, g1, w_in_b, cos, sin):
    n, d = x2d.shape
    tm = PROJ_TM
    return pl.pallas_call(
        _inproj_kernel,
        grid=(n // tm,),
        in_specs=[pl.BlockSpec((tm, d), lambda i: (i, 0)),
                  pl.BlockSpec((1, d), lambda i: (0, 0)),
                  pl.BlockSpec((d, PROJ_WIDTH), lambda i: (0, 0)),
                  pl.BlockSpec((tm, LANES), lambda i: (i, 0)),
                  pl.BlockSpec((tm, LANES), lambda i: (i, 0))],
        out_specs=pl.BlockSpec((tm, PROJ_WIDTH), lambda i: (i, 0)),
        out_shape=jax.ShapeDtypeStruct((n, PROJ_WIDTH), BF16),
        compiler_params=_cparams("arbitrary"),
        name="in_proj",
    )(x2d, g1.reshape(1, d), w_in_b, cos, sin)


def _diff_attn_kernel(lq1_ref, lk1_ref, lq2_ref, lk2_ref, g_ref, q_ref, k_ref, v_ref, o_ref,
                      qq_sc, m_sc, l_sc, acc_sc, *, lam_init):
    tq, tk = ATT_TQ, ATT_TK
    i = pl.program_id(2)
    q = q_ref[...]
    lane = lax.broadcasted_iota(I32, q.shape, 1)
    zero = jnp.zeros_like(q)
    qq_sc[:tq, :] = jnp.where(lane < DIFF_QK_DIM, q, zero)
    qq_sc[tq:, :] = jnp.where(lane >= DIFF_QK_DIM, q, zero)
    m_sc[...] = jnp.full(m_sc.shape, NEG_BIG, F32)
    l_sc[...] = jnp.zeros(l_sc.shape, F32)
    acc_sc[...] = jnp.zeros(acc_sc.shape, F32)

    def step(start, width, masked):
        k = k_ref[pl.ds(start, width), :]
        v = v_ref[pl.ds(start, width), :]
        for part in range(ATT_PARTS):
            rows = slice(part * (2 * tq // ATT_PARTS), (part + 1) * (2 * tq // ATT_PARTS))
            s = lax.dot_general(qq_sc[rows, :], k, (((1,), (1,)), ((), ())), preferred_element_type=F32)
            if masked:
                row = (lax.broadcasted_iota(I32, s.shape, 0) + rows.start) & (tq - 1)
                col = lax.broadcasted_iota(I32, s.shape, 1)
                s = jnp.where(col <= row, s, NEG_BIG)
            m_prev = m_sc[rows, :]
            m_next = jnp.maximum(m_prev, jnp.max(s, axis=1, keepdims=True))
            alpha = jnp.exp2(m_prev - m_next)
            p = jnp.exp2(s - jnp.concatenate([m_next] * (width // LANES), axis=1))
            psum = p[:, :LANES]
            for c in range(1, width // LANES):
                psum = psum + p[:, c * LANES:(c + 1) * LANES]
            l_sc[rows, :] = alpha * l_sc[rows, :] + psum
            acc_sc[rows, :] = alpha * acc_sc[rows, :] + jnp.dot(p.astype(BF16), v, preferred_element_type=F32)
            m_sc[rows, :] = m_next

    def body(j, carry):
        step(pl.multiple_of(j * tk, tk), tk, False)
        return carry

    lax.fori_loop(0, i, body, 0)
    step(pl.multiple_of(i * tk, tk), tk, True)

    lam = (jnp.exp(jnp.sum(lq1_ref[...] * lk1_ref[...], axis=-1, keepdims=True))
           - jnp.exp(jnp.sum(lq2_ref[...] * lk2_ref[...], axis=-1, keepdims=True))
           + lam_init)
    o = acc_sc[...] / jnp.sum(l_sc[...], axis=1, keepdims=True)
    d = o[:tq, :] - lam * o[tq:, :]
    ms = jnp.mean(d * d, axis=-1, keepdims=True)
    out = d * lax.rsqrt(ms + RMS_EPS) * g_ref[...] * (1.0 - lam_init)
    o_ref[...] = out.astype(BF16)


def _diff_attention(proj, lq1, lk1, lq2, lk2, g, batch, seq, lam_init):
    n = batch * seq
    tq = ATT_TQ
    assert ATT_TQ == ATT_TK and seq % tq == 0 and DIFF_V_DIM == LANES
    nq = seq // tq
    qcol = COL_DQ // LANES
    kcol = COL_DK // LANES
    vcol = COL_DV // LANES
    vec = lambda b, h, i: (0, 0)
    return pl.pallas_call(
        functools.partial(_diff_attn_kernel, lam_init=lam_init),
        grid=(batch, N_DIFF_HEADS, nq),
        in_specs=[pl.BlockSpec((1, DIFF_QK_DIM), vec)] * 4 + [
            pl.BlockSpec((1, DIFF_V_DIM), vec),
            pl.BlockSpec((tq, LANES), lambda b, h, i: (b * nq + i, qcol + h)),
            pl.BlockSpec((seq, LANES), lambda b, h, i: (b, kcol + h)),
            pl.BlockSpec((seq, LANES), lambda b, h, i: (b, vcol + h))],
        out_specs=pl.BlockSpec((tq, LANES), lambda b, h, i: (b * nq + i, h)),
        out_shape=jax.ShapeDtypeStruct((n, DIFF_WIDTH), BF16),
        scratch_shapes=[pltpu.VMEM((2 * tq, LANES), BF16),
                        pltpu.VMEM((2 * tq, LANES), F32),
                        pltpu.VMEM((2 * tq, LANES), F32),
                        pltpu.VMEM((2 * tq, DIFF_V_DIM), F32)],
        compiler_params=_cparams("arbitrary", "arbitrary", "arbitrary"),
        name="diff_attn",
    )(lq1.reshape(1, -1), lk1.reshape(1, -1), lq2.reshape(1, -1), lk2.reshape(1, -1),
      g.reshape(1, -1), proj, proj, proj)


def _retention_kernel(q_ref, k_ref, v_ref, gate_ref, g_ref, o_ref, state_sc):
    c_len = RET_C

    @pl.when(pl.program_id(1) == 0)
    def _():
        state_sc[...] = jnp.zeros(state_sc.shape, F32)

    ii = lax.broadcasted_iota(I32, (c_len, c_len), 0)
    jj = lax.broadcasted_iota(I32, (c_len, c_len), 1)
    rel = (ii - jj).astype(F32)
    lane = lax.broadcasted_iota(I32, (c_len, LANES), 1)
    pos = lax.broadcasted_iota(I32, (c_len, LANES), 0).astype(F32)
    srow = lax.broadcasted_iota(I32, (LANES, LANES), 0)
    for pair in range(N_RET_HEADS // 2):
        ld = (LOG_DECAY[2 * pair], LOG_DECAY[2 * pair + 1])
        ld_lane = jnp.where(lane < RET_QK_DIM, ld[0], ld[1])
        q_decay = jnp.exp(ld_lane * (pos + 1.0))
        k_decay = jnp.exp(ld_lane * (c_len - 1.0 - pos))
        chunk_decay = jnp.where(srow < RET_QK_DIM, math.exp(ld[0] * c_len), math.exp(ld[1] * c_len))
        intra = [jnp.where(rel >= 0, jnp.exp(l * jnp.maximum(rel, 0.0)), 0.0) for l in ld]
        in_head = (lane < RET_QK_DIM, lane >= RET_QK_DIM)
        for c in range(RET_T // c_len):
            rows = slice(c * c_len, (c + 1) * c_len)
            qb = q_ref[rows, pair * LANES:(pair + 1) * LANES]
            kb = k_ref[rows, pair * LANES:(pair + 1) * LANES]
            q = qb.astype(F32)
            state = state_sc[pair]
            state_b = state.astype(BF16)
            kd_t = (kb.astype(F32) * k_decay).T.astype(BF16)
            new_kv = []
            for hh in range(2):
                h = 2 * pair + hh
                qm = jnp.where(in_head[hh], q, 0.0)
                s = lax.dot_general(qm.astype(BF16), kb, (((1,), (1,)), ((), ())),
                                    preferred_element_type=F32) * intra[hh]
                v = v_ref[rows, h * RET_V_DIM:(h + 1) * RET_V_DIM]
                y = (jnp.dot(s.astype(BF16), v, preferred_element_type=F32)
                     + jnp.dot((qm * q_decay).astype(BF16), state_b, preferred_element_type=F32))
                new_kv.append(jnp.dot(kd_t, v, preferred_element_type=F32))
                ms = jnp.mean(y * y, axis=-1, keepdims=True)
                yn = y * lax.rsqrt(ms + RMS_EPS) * g_ref[h:h + 1, :]
                gate = gate_ref[rows, h * RET_V_DIM:(h + 1) * RET_V_DIM].astype(F32)
                o_ref[rows, h * RET_V_DIM:(h + 1) * RET_V_DIM] = (yn * gate).astype(BF16)
            state_sc[pair] = chunk_decay * state + jnp.where(srow < RET_QK_DIM, new_kv[0], new_kv[1])


def _retention(proj, g, batch, seq):
    n = batch * seq
    t = RET_T
    nt = seq // t
    qk_w = N_RET_HEADS * RET_QK_DIM
    return pl.pallas_call(
        _retention_kernel,
        grid=(batch, nt),
        in_specs=[pl.BlockSpec((t, qk_w), lambda b, i: (b * nt + i, COL_RQ // qk_w)),
                  pl.BlockSpec((t, qk_w), lambda b, i: (b * nt + i, COL_RK // qk_w)),
                  pl.BlockSpec((t, RET_WIDTH), lambda b, i: (b * nt + i, COL_RV // RET_WIDTH)),
                  pl.BlockSpec((t, RET_WIDTH), lambda b, i: (b * nt + i, COL_RG // RET_WIDTH)),
                  pl.BlockSpec((N_RET_HEADS, RET_V_DIM), lambda b, i: (0, 0))],
        out_specs=pl.BlockSpec((t, RET_WIDTH), lambda b, i: (b * nt + i, 0)),
        out_shape=jax.ShapeDtypeStruct((n, RET_WIDTH), BF16),
        scratch_shapes=[pltpu.VMEM((N_RET_HEADS // 2, LANES, RET_V_DIM), F32)],
        compiler_params=_cparams("arbitrary", "arbitrary"),
        name="retention",
    )(proj, proj, proj, proj, g)


def _outproj_kernel(d_ref, r_ref, wo_ref, x_ref, g2_ref, wr_ref, br_ref,
                    x1_ref, h2p_ref, idx_ref, gate_ref):
    acc = (jnp.dot(d_ref[...], wo_ref[:DIFF_WIDTH, :], preferred_element_type=F32)
           + jnp.dot(r_ref[...], wo_ref[DIFF_WIDTH:, :], preferred_element_type=F32))
    x1 = x_ref[...] + acc
    x1_ref[...] = x1
    ms = jnp.mean(x1 * x1, axis=-1, keepdims=True)
    h2 = x1 * lax.rsqrt(ms + RMS_EPS) * g2_ref[...]
    half = h2.shape[1] // 2
    packed = pltpu.pack_elementwise([h2[:, :half], h2[:, half:]], packed_dtype=BF16)
    slabs = half // LANES
    for j in range(slabs):
        h2p_ref[pl.ds(j, packed.shape[0], stride=slabs), :] = packed[:, j * LANES:(j + 1) * LANES]

    nt = (((1,), (1,)), ((), ()))
    wr = wr_ref[...]
    wr_hi = wr.astype(BF16)
    wr_lo = (wr - wr_hi.astype(F32)).astype(BF16)
    h2_hi = h2.astype(BF16)
    h2_lo = (h2 - h2_hi.astype(F32)).astype(BF16)
    logits = (lax.dot_general(wr_hi, h2_hi, nt, preferred_element_type=F32)
              + lax.dot_general(wr_hi, h2_lo, nt, preferred_element_type=F32)
              + lax.dot_general(wr_lo, h2_hi, nt, preferred_element_type=F32)
              + br_ref[...])
    e_iota = lax.broadcasted_iota(I32, logits.shape, 0)
    vals = []
    for r in range(TOP_K):
        m = jnp.max(logits, axis=0, keepdims=True)
        ix = jnp.min(jnp.where(logits == m, e_iota, N_EXPERTS), axis=0, keepdims=True)
        vals.append(m)
        idx_ref[r:r + 1, :] = ix
        logits = jnp.where(e_iota == ix, -jnp.inf, logits)
    ex = [jnp.exp(v - vals[0]) for v in vals]
    den = ex[0] + ex[1] + ex[2] + ex[3]
    for r in range(TOP_K):
        gate_ref[r:r + 1, :] = ex[r] / den


def _out_proj(d_out, r_out, w_o_b, x2d, g2, w_router_t, b_router):
    n, d = x2d.shape
    tm = OUT_TM
    const = lambda i: (0, 0)
    return pl.pallas_call(
        _outproj_kernel,
        grid=(n // tm,),
        in_specs=[pl.BlockSpec((tm, DIFF_WIDTH), lambda i: (i, 0)),
                  pl.BlockSpec((tm, RET_WIDTH), lambda i: (i, 0)),
                  pl.BlockSpec((DIFF_WIDTH + RET_WIDTH, d), const),
                  pl.BlockSpec((tm, d), lambda i: (i, 0)),
                  pl.BlockSpec((1, d), const),
                  pl.BlockSpec((N_EXPERTS, d), const),
                  pl.BlockSpec((N_EXPERTS, 1), const)],
        out_specs=[pl.BlockSpec((tm, d), lambda i: (i, 0)),
                   pl.BlockSpec((tm * (d // 2 // LANES), LANES), lambda i: (i, 0)),
                   pl.BlockSpec((TOP_K, tm), lambda i: (0, i)),
                   pl.BlockSpec((TOP_K, tm), lambda i: (0, i))],
        out_shape=[jax.ShapeDtypeStruct((n, d), F32),
                   jax.ShapeDtypeStruct((n * (d // 2 // LANES), LANES), U32),
                   jax.ShapeDtypeStruct((TOP_K, n), I32),
                   jax.ShapeDtypeStruct((TOP_K, n), F32)],
        compiler_params=_cparams("arbitrary"),
        name="out_proj",
    )(d_out, r_out, w_o_b, x2d, g2.reshape(1, d), w_router_t, b_router.reshape(N_EXPERTS, 1))


def _plan_kernel(idx_ref, dest_ref, blk_ref, ends_ref, cnt_sc, base_sc, tri_sc, *, nb_pad):
    ph = pl.program_id(0)
    c = pl.program_id(1)
    tc = PLAN_TC
    e_iota = lax.broadcasted_iota(I32, (N_EXPERTS, tc), 0)

    @pl.when((ph == 0) & (c == 0))
    def _():
        cnt_sc[...] = jnp.zeros(cnt_sc.shape, F32)
        s = lax.broadcasted_iota(I32, (tc, tc), 0)
        t = lax.broadcasted_iota(I32, (tc, tc), 1)
        tri_sc[...] = jnp.where(s < t, 1.0, 0.0).astype(BF16)

    @pl.when(ph == 0)
    def _():
        tot = jnp.zeros((N_EXPERTS, 1), F32)
        for k in range(TOP_K):
            oh = idx_ref[k:k + 1, :] == e_iota
            tot = tot + jnp.sum(jnp.where(oh, 1.0, 0.0), axis=1, keepdims=True)
        cnt_sc[...] = cnt_sc[...] + tot

    @pl.when((ph == 1) & (c == 0))
    def _():
        cnt = cnt_sc[...]
        nblk = jnp.floor((cnt + (MOE_BM - 1.0)) * (1.0 / MOE_BM))
        ei = lax.broadcasted_iota(I32, (N_EXPERTS, LANES), 0)
        li = lax.broadcasted_iota(I32, (N_EXPERTS, LANES), 1)
        nblk_row = jnp.sum(jnp.where(ei == li, nblk, 0.0), axis=0, keepdims=True)
        start = jnp.sum(jnp.where(li < ei, nblk_row, 0.0), axis=1, keepdims=True)
        base_sc[...] = start * MOE_BM
        end = start + nblk
        bi = lax.broadcasted_iota(I32, (N_EXPERTS, nb_pad), 1).astype(F32)
        be = jnp.sum(jnp.where(end <= bi, 1.0, 0.0), axis=0, keepdims=True)
        blk_ref[...] = jnp.minimum(be, N_EXPERTS - 1.0).astype(I32)
        ends_ref[...] = jnp.sum(jnp.where(ei == li, end, 0.0), axis=0, keepdims=True).astype(I32)

    @pl.when(ph == 1)
    def _():
        base = base_sc[...]
        for k in range(TOP_K):
            oh = idx_ref[k:k + 1, :] == e_iota
            ohb = jnp.where(oh, 1.0, 0.0).astype(BF16)
            before = jnp.dot(ohb, tri_sc[...], preferred_element_type=F32)
            rank = jnp.sum(jnp.where(oh, before + base, 0.0), axis=0, keepdims=True)
            dest_ref[k:k + 1, :] = rank.astype(I32)
            base = base + jnp.sum(jnp.where(oh, 1.0, 0.0), axis=1, keepdims=True)
        base_sc[...] = base


def _plan(top_idx, nb_pad):
    n = top_idx.shape[1]
    tc = PLAN_TC
    return pl.pallas_call(
        functools.partial(_plan_kernel, nb_pad=nb_pad),
        grid=(2, n // tc),
        in_specs=[pl.BlockSpec((TOP_K, tc), lambda ph, c: (0, c))],
        out_specs=[pl.BlockSpec((TOP_K, tc), lambda ph, c: (0, c * ph)),
                   pl.BlockSpec((1, nb_pad), lambda ph, c: (0, 0)),
                   pl.BlockSpec((1, LANES), lambda ph, c: (0, 0))],
        out_shape=[jax.ShapeDtypeStruct((TOP_K, n), I32),
                   jax.ShapeDtypeStruct((1, nb_pad), I32),
                   jax.ShapeDtypeStruct((1, LANES), I32)],
        scratch_shapes=[pltpu.VMEM((N_EXPERTS, 1), F32),
                        pltpu.VMEM((N_EXPERTS, 1), F32),
                        pltpu.VMEM((tc, tc), BF16)],
        compiler_params=_cparams("arbitrary", "arbitrary"),
        name="plan",
    )(top_idx)


def _dispatch_kernel(ends_ref, dest_ref, h_ref, xbuf_ref, zbuf, sem, zsem, *, nb):
    tm = DISP_TM
    bm = MOE_BM

    @pl.when(pl.program_id(0) == 0)
    def _():
        zbuf[...] = jnp.zeros(zbuf.shape, zbuf.dtype)

        def zero_block(blk):
            return pltpu.make_async_copy(zbuf, xbuf_ref.at[pl.ds(pl.multiple_of(blk * bm, bm), bm)], zsem)

        def per_block(fn):
            for e in range(N_EXPERTS):
                end = ends_ref[e]
                first = ends_ref[e - 1] if e else 0

                @pl.when(end > first)
                def _():
                    fn(zero_block(end - 1))

            def tail(blk, carry):
                fn(zero_block(blk))
                return carry

            lax.fori_loop(ends_ref[N_EXPERTS - 1], nb, tail, 0)

        per_block(lambda cp: cp.start())
        per_block(lambda cp: cp.wait())

    def issue(r, carry):
        for u in range(TOK_PER_ROW):
            for k in range(TOP_K):
                src = h_ref.at[r * TOK_PER_ROW + u]
                dst = xbuf_ref.at[dest_ref[r, u * TOP_K + k]]
                pltpu.make_async_copy(src, dst, sem).start(priority=k % 2)
        return carry

    lax.fori_loop(0, tm // TOK_PER_ROW, issue, 0)
    for k in range(TOP_K):
        pltpu.make_async_copy(xbuf_ref.at[pl.ds(0, tm)], xbuf_ref.at[pl.ds(0, tm)], sem).wait()


def _dispatch(ends, dest, h2p, p_rows):
    n, s, w = h2p.shape
    tm = DISP_TM
    grid_spec = pltpu.PrefetchScalarGridSpec(
        num_scalar_prefetch=1,
        grid=(n // tm,),
        in_specs=[pl.BlockSpec((tm // TOK_PER_ROW, LANES), lambda i, ends: (i, 0), memory_space=pltpu.SMEM),
                  pl.BlockSpec((tm, s, w), lambda i, ends: (i, 0, 0))],
        out_specs=pl.BlockSpec(memory_space=pl.ANY),
        scratch_shapes=[pltpu.VMEM((MOE_BM, s, w), h2p.dtype),
                        pltpu.SemaphoreType.DMA(()),
                        pltpu.SemaphoreType.DMA(())],
    )
    return pl.pallas_call(
        functools.partial(_dispatch_kernel, nb=p_rows // MOE_BM),
        grid_spec=grid_spec,
        out_shape=jax.ShapeDtypeStruct((p_rows, s, w), h2p.dtype),
        compiler_params=pltpu.CompilerParams(dimension_semantics=("arbitrary",),
                                             vmem_limit_bytes=VMEM_LIMIT,
                                             has_side_effects=True),
        name="dispatch",
    )(ends, dest, h2p)


def _pair_perm():
    a = lax.broadcasted_iota(I32, (MXU_DIM, MXU_DIM), 0)
    b = lax.broadcasted_iota(I32, (MXU_DIM, MXU_DIM), 1)
    src = jnp.where(b < LANES, 2 * b, 2 * (b - LANES) + 1)
    return jnp.where(a == src, 1.0, 0.0).astype(BF16)


def _pair_group(b):
    e, f2 = b.shape
    return b.reshape(e, f2 // MXU_DIM, LANES, 2).transpose(0, 1, 3, 2).reshape(e, 1, f2)


def _expert_kernel(blk_ref, ends_ref, x_ref, w1_hbm, b1_ref, w2_hbm, b2_ref, y_ref,
                   w1f, w2f, w1_ref, w2_ref, wsem, slot_ref):
    b = pl.program_id(0)
    n_used = ends_ref[N_EXPERTS - 1]

    @pl.when(b >= n_used)
    def _():
        zero = jnp.zeros(y_ref.shape, F32)
        y_ref[...] = pltpu.pack_elementwise([zero, zero], packed_dtype=BF16)

    def fetch(expert, slot):
        return (pltpu.make_async_copy(w1_hbm.at[expert], w1f.at[slot], wsem.at[slot]),
                pltpu.make_async_copy(w2_hbm.at[expert], w2f.at[slot], wsem.at[slot]))

    @pl.when(b == 0)
    def _():
        slot_ref[0] = 0
        for cp in fetch(blk_ref[0], 0):
            cp.start()

    new_expert = (b == 0) | (blk_ref[b] != blk_ref[jnp.maximum(b - 1, 0)])

    @pl.when((b < n_used) & new_expert)
    def _():
        slot = slot_ref[0]
        expert = blk_ref[b]
        for cp in fetch(expert, slot):
            cp.wait()
        next_first = ends_ref[expert]

        @pl.when(next_first < n_used)
        def _():
            for cp in fetch(blk_ref[next_first], 1 - slot):
                cp.start()

        perm = _pair_perm()
        for c in range(w1f.shape[2] // MXU_DIM):
            cols = slice(c * MXU_DIM, (c + 1) * MXU_DIM)
            blk = w1f[slot, :, cols].astype(BF16)
            w1_ref[0, :, cols] = jnp.dot(blk, perm, preferred_element_type=F32).astype(BF16)
        w2_ref[0] = w2f[slot].astype(BF16)
        slot_ref[0] = 1 - slot

    @pl.when(b < n_used)
    def _():
        ns = w2_ref.shape[2] // 2 // LANES
        bm = x_ref.shape[0] // ns
        slabs = [x_ref[pl.ds(j, bm, stride=ns), :] for j in range(ns)]
        lo = [pltpu.unpack_elementwise(w, index=0, packed_dtype=BF16, unpacked_dtype=F32).astype(BF16)
              for w in slabs]
        hi = [pltpu.unpack_elementwise(w, index=1, packed_dtype=BF16, unpacked_dtype=F32).astype(BF16)
              for w in slabs]
        x = jnp.concatenate(lo + hi, axis=1)
        acc = jnp.zeros((x.shape[0], w2_ref.shape[2]), F32)
        grp = 2 * MXU_DIM
        for c in range(w1_ref.shape[2] // grp):
            cols = slice(c * grp, (c + 1) * grp)
            h = jnp.dot(x, w1_ref[0, :, cols], preferred_element_type=F32) + b1_ref[0, :, cols]
            glu = jnp.concatenate([h[:, 0:LANES], h[:, 2 * LANES:3 * LANES]], axis=1)
            lin = jnp.concatenate([h[:, LANES:2 * LANES], h[:, 3 * LANES:4 * LANES]], axis=1)
            glu = jnp.minimum(glu, SWIGLU_LIMIT)
            lin = jnp.clip(lin, -SWIGLU_LIMIT, SWIGLU_LIMIT)
            act = glu * jax.nn.sigmoid(SWIGLU_ALPHA * glu) * (lin + 1.0)
            acc = acc + jnp.dot(act.astype(BF16), w2_ref[0, c * MXU_DIM:(c + 1) * MXU_DIM, :],
                                preferred_element_type=F32)
        y = acc + b2_ref[0]
        half = y.shape[1] // 2
        packed = pltpu.pack_elementwise([y[:, :half], y[:, half:]], packed_dtype=BF16)
        for j in range(ns):
            y_ref[pl.ds(j, bm, stride=ns), :] = packed[:, j * LANES:(j + 1) * LANES]


def _experts(blk_expert, ends, x_buf, w1, b1, w2, b2):
    e, d, f2 = w1.shape
    xs = d // 2 // LANES
    ys = xs
    p_rows = x_buf.shape[0] // xs
    f = f2 // 2
    bm = MOE_BM
    nb = p_rows // bm

    def row_blk(b, ends):
        return jnp.minimum(b, ends[N_EXPERTS - 1] - 1)

    grid_spec = pltpu.PrefetchScalarGridSpec(
        num_scalar_prefetch=2,
        grid=(nb,),
        in_specs=[pl.BlockSpec((bm * xs, LANES), lambda b, blk, ends: (row_blk(b, ends), 0)),
                  pl.BlockSpec(memory_space=pl.ANY),
                  pl.BlockSpec((1, 1, f2), lambda b, blk, ends: (blk[row_blk(b, ends)], 0, 0)),
                  pl.BlockSpec(memory_space=pl.ANY),
                  pl.BlockSpec((1, 1, d), lambda b, blk, ends: (blk[row_blk(b, ends)], 0, 0))],
        out_specs=pl.BlockSpec((bm * ys, LANES), lambda b, blk, ends: (b, 0)),
        scratch_shapes=[pltpu.VMEM((2, d, f2), F32),
                        pltpu.VMEM((2, f, d), F32),
                        pltpu.VMEM((1, d, f2), BF16),
                        pltpu.VMEM((1, f, d), BF16),
                        pltpu.SemaphoreType.DMA((2,)),
                        pltpu.SMEM((1,), I32)],
    )
    return pl.pallas_call(
        _expert_kernel,
        grid_spec=grid_spec,
        out_shape=jax.ShapeDtypeStruct((p_rows * ys, LANES), x_buf.dtype),
        compiler_params=_cparams("arbitrary"),
        name="experts",
    )(blk_expert, ends, x_buf, w1, b1, w2, b2)


def _combine_kernel(dest_ref, next_dest_ref, gate_ref, x1_ref, gf_ref, ybuf_ref, o_ref, gbuf, sem):
    tm = COMB_TM
    i = pl.program_id(0)
    cur = i % 2

    ys = ybuf_ref.shape[1]

    def gather(idx_ref, buf):
        def issue(r, carry):
            for u in range(TOK_PER_ROW):
                for k in range(TOP_K):
                    src = ybuf_ref.at[idx_ref[r, u * TOP_K + k]]
                    row = pl.multiple_of(r * (TOK_PER_ROW * ys) + u * ys, ys)
                    dst = gbuf.at[buf, k, pl.ds(row, ys), :]
                    pltpu.make_async_copy(src, dst, sem.at[buf]).start(priority=k % 2)
            return carry

        lax.fori_loop(0, tm // TOK_PER_ROW, issue, 0)

    @pl.when(i == 0)
    def _():
        gather(dest_ref, 0)

    @pl.when(i + 1 < pl.num_programs(0))
    def _():
        gather(next_dest_ref, 1 - cur)

    for k in range(TOP_K):
        pltpu.make_async_copy(ybuf_ref.at[pl.ds(0, tm)], ybuf_ref.at[pl.ds(0, tm)], sem.at[cur]).wait()

    gates = gate_ref[...]
    pad = jnp.zeros((LANES - TOP_K, LANES), F32)
    cols = []
    for c in range(tm // LANES):
        blk = jnp.concatenate([gates[:, c * LANES:(c + 1) * LANES], pad], axis=0)
        cols.append(blk.T)
    gcol = jnp.concatenate(cols, axis=0)
    for c in range(tm // COMB_CH):
        rows = slice(c * COMB_CH, (c + 1) * COMB_CH)
        gk = [gcol[rows, k:k + 1] for k in range(TOP_K)]
        sq = jnp.zeros((COMB_CH, LANES), F32)
        for j in range(ys):
            words = [gbuf[cur, k, pl.ds(c * COMB_CH * ys + j, COMB_CH, stride=ys), :] for k in range(TOP_K)]
            for part in range(2):
                col = (part * ys + j) * LANES
                a = x1_ref[rows, col:col + LANES]
                for k in range(TOP_K):
                    yk = pltpu.unpack_elementwise(words[k], index=part, packed_dtype=BF16, unpacked_dtype=F32)
                    a = a + yk * gk[k]
                sq = sq + a * a
                o_ref[rows, col:col + LANES] = a
        ms = jnp.sum(sq, axis=-1, keepdims=True) * (1.0 / o_ref.shape[1])
        o_ref[rows, :] = o_ref[rows, :] * lax.rsqrt(ms + RMS_EPS) * gf_ref[...]


def _combine(dest, gates, x1, gf, y_buf):
    n, d = x1.shape
    tm = COMB_TM
    last = n // tm - 1
    idx_rows = tm // TOK_PER_ROW
    return pl.pallas_call(
        _combine_kernel,
        grid=(n // tm,),
        in_specs=[pl.BlockSpec((idx_rows, LANES), lambda i: (i, 0), memory_space=pltpu.SMEM),
                  pl.BlockSpec((idx_rows, LANES), lambda i: (jnp.minimum(i + 1, last), 0), memory_space=pltpu.SMEM),
                  pl.BlockSpec((TOP_K, tm), lambda i: (0, i)),
                  pl.BlockSpec((tm, d), lambda i: (i, 0)),
                  pl.BlockSpec((1, d), lambda i: (0, 0)),
                  pl.BlockSpec(memory_space=pl.ANY)],
        out_specs=pl.BlockSpec((tm, d), lambda i: (i, 0)),
        out_shape=jax.ShapeDtypeStruct((n, d), F32),
        scratch_shapes=[pltpu.VMEM((2, TOP_K, tm * y_buf.shape[1], LANES), y_buf.dtype),
                        pltpu.SemaphoreType.DMA((2,))],
        compiler_params=_cparams("arbitrary"),
        name="combine",
    )(dest, dest, gates, x1, gf.reshape(1, d), y_buf)


def kernel(x, positions, norm1_g, w_in, lambda_q1, lambda_k1, lambda_q2, lambda_k2, diff_norm_g, ret_norm_g, w_o, norm2_g, w_router, b_router, w_moe_in, b_moe_in, w_moe_out, b_moe_out, norm_f_g):
    batch, seq, d = x.shape
    n = batch * seq
    assert norm1_g.shape[0] == 1, "single-layer block"
    l = 0
    p_rows = n * TOP_K + N_EXPERTS * MOE_BM
    nb = p_rows // MOE_BM
    nb_pad = -(-nb // LANES) * LANES

    cos, sin = _rope_tables(positions)
    x2d = x.reshape(n, d)
    lam_init = 0.8 - 0.6 * math.exp(-0.3 * l)
    proj = _in_proj(x2d, norm1_g[l], w_in[l].astype(BF16), cos, sin)
    d_out = _diff_attention(proj, lambda_q1[l], lambda_k1[l], lambda_q2[l], lambda_k2[l],
                            diff_norm_g[l], batch, seq, lam_init)
    r_out = _retention(proj, ret_norm_g[l], batch, seq)
    x1, h2p, top_idx, gates = _out_proj(d_out, r_out, w_o[l].astype(BF16), x2d, norm2_g[l],
                                        w_router[l].T, b_router[l])
    dest, blk_expert, ends = _plan(top_idx, nb_pad)
    ends = ends.reshape(-1)
    dest = dest.T.reshape(n // TOK_PER_ROW, LANES)
    xs = ys = d // 2 // LANES
    x_buf = _dispatch(ends, dest, h2p.reshape(n, xs, LANES), p_rows)
    b1 = _pair_group(b_moe_in[l])
    b2 = b_moe_out[l].reshape(N_EXPERTS, 1, -1)
    y_buf = _experts(blk_expert.reshape(-1), ends, x_buf.reshape(p_rows * xs, LANES),
                     w_moe_in[l], b1, w_moe_out[l], b2)
    out = _combine(dest, gates, x1, norm_f_g, y_buf.reshape(p_rows, ys, LANES))
    return out.reshape(batch, seq, d)
```

```python
import functools
import math

import numpy as np
import jax
import jax.numpy as jnp
from jax import lax
from jax.experimental import pallas as pl
from jax.experimental.pallas import tpu as pltpu

F32 = jnp.float32
BF16 = jnp.bfloat16
I32 = jnp.int32
U32 = jnp.uint32

N_DIFF_HEADS = 4
DIFF_QK_DIM = 64
DIFF_V_DIM = 128
DIFF_WIDTH = N_DIFF_HEADS * DIFF_V_DIM
N_RET_HEADS = 4
RET_QK_DIM = 64
RET_V_DIM = 128
RET_WIDTH = N_RET_HEADS * RET_V_DIM
ROPE_THETA = 10000.0
RMS_EPS = 1e-5
N_EXPERTS = 32
TOP_K = 4
SWIGLU_LIMIT = 7.0
SWIGLU_ALPHA = 1.702
TOK_PER_ROW = 128 // TOP_K
COL_DQ = 0
COL_DK = 512
COL_DV = 1024
COL_RQ = 1536
COL_RK = 1792
COL_RV = 2048
COL_RG = 2560
PROJ_WIDTH = 3072
LOG_DECAY = tuple(math.log(1.0 - 2.0 ** (-5.0 - h)) for h in range(N_RET_HEADS))
LOG2E = 1.4426950408889634

LANES = 128
SUBLANES = 8
MXU_DIM = 256
VMEM_LIMIT = 56 * 1024 * 1024

ROPE_ROWS = 1024
PROJ_TM = 512
ATT_TQ = 1024
ATT_TK = 1024
RET_T = 512
RET_C = 256
OUT_TM = 1024
PLAN_TC = 512
DISP_TM = 1024
MOE_BM = 512
COMB_TM = 512
COMB_CH = 64

NEG_BIG = -1e30


def _cparams(*sem):
    return pltpu.CompilerParams(dimension_semantics=sem, vmem_limit_bytes=VMEM_LIMIT)


def _rope_table_kernel(pos_ref, invf_ref, cos_ref, sin_ref):
    invf = invf_ref[...]
    lane = lax.broadcasted_iota(I32, (LANES, LANES), 1)
    first_half = (lane & 32) == 0
    nf = DIFF_QK_DIM // 2
    groups = LANES // nf
    for r0 in range(0, ROPE_ROWS // LANES, groups):
        stacked = jnp.concatenate(
            [jnp.broadcast_to(pos_ref[r0 + g:r0 + g + 1, :].astype(F32), (nf, LANES)) for g in range(groups)],
            axis=0)
        ang = stacked.T * invf
        c4 = jnp.cos(ang)
        s4 = jnp.sin(ang)
        for g in range(groups):
            mine = (lane // nf) == g

            def spread(t):
                z = jnp.where(mine, t, 0.0)
                return z + pltpu.roll(z, nf, 1) + pltpu.roll(z, 2 * nf, 1) + pltpu.roll(z, 3 * nf, 1)

            rows = slice((r0 + g) * LANES, (r0 + g + 1) * LANES)
            s = spread(s4)
            cos_ref[rows, :] = spread(c4)
            sin_ref[rows, :] = jnp.where(first_half, -s, s)


def _rope_tables(positions):
    n = positions.size
    pos2d = positions.reshape(n // LANES, LANES)
    d = DIFF_QK_DIM
    inv_freq = 1.0 / (ROPE_THETA ** (jnp.arange(0, d, 2, dtype=F32) / d))
    invf = jnp.tile(inv_freq, LANES // (d // 2)).reshape(1, LANES)
    rows = ROPE_ROWS // LANES
    return pl.pallas_call(
        _rope_table_kernel,
        grid=(n // ROPE_ROWS,),
        in_specs=[pl.BlockSpec((rows, LANES), lambda i: (i, 0)),
                  pl.BlockSpec((1, LANES), lambda i: (0, 0))],
        out_specs=[pl.BlockSpec((ROPE_ROWS, LANES), lambda i: (i, 0)),
                   pl.BlockSpec((ROPE_ROWS, LANES), lambda i: (i, 0))],
        out_shape=[jax.ShapeDtypeStruct((n, LANES), F32)] * 2,
        compiler_params=_cparams("arbitrary"),
        name="rope_table",
    )(pos2d, invf)


def _inproj_kernel(x_ref, g_ref, w_ref, cos_ref, sin_ref, o_ref):
    x = x_ref[...]
    ms = jnp.mean(x * x, axis=-1, keepdims=True)
    h = (x * lax.rsqrt(ms + RMS_EPS) * g_ref[...]).astype(BF16)
    cos = cos_ref[...]
    sin = sin_ref[...]
    lane = lax.broadcasted_iota(I32, cos.shape, 1)
    first_half = (lane & 32) == 0

    def rope(t):
        rot = jnp.where(first_half, pltpu.roll(t, 96, 1), pltpu.roll(t, 32, 1))
        return t * cos + rot * sin

    q_scale = DIFF_QK_DIM ** -0.5 * LOG2E
    k_scale = RET_QK_DIM ** -0.5
    for c in range(PROJ_WIDTH // MXU_DIM):
        p = jnp.dot(h, w_ref[:, c * MXU_DIM:(c + 1) * MXU_DIM], preferred_element_type=F32)
        for half in range(MXU_DIM // LANES):
            col = c * MXU_DIM + half * LANES
            t = p[:, half * LANES:(half + 1) * LANES]
            if col < COL_DK:
                t = rope(t) * q_scale
            elif col < COL_DV:
                t = rope(t)
            elif col < COL_RQ:
                pass
            elif col < COL_RK:
                t = rope(t)
            elif col < COL_RV:
                t = rope(t) * k_scale
            elif col < COL_RG:
                pass
            else:
                t = t * jax.nn.sigmoid(t)
            o_ref[:, col:col + LANES] = t.astype(BF16)


def _in_proj(x2d, g1, w_in_b, cos, sin):
    n, d = x2d.shape
    tm = PROJ_TM
    return pl.pallas_call(
        _inproj_kernel,
        grid=(n // tm,),
        in_specs=[pl.BlockSpec((tm, d), lambda i: (i, 0)),
                  pl.BlockSpec((1, d), lambda i: (0, 0)),
                  pl.BlockSpec((d, PROJ_WIDTH), lambda i: (0, 0)),
                  pl.BlockSpec((tm, LANES), lambda i: (i, 0)),
                  pl.BlockSpec((tm, LANES), lambda i: (i, 0))],
        out_specs=pl.BlockSpec((tm, PROJ_WIDTH), lambda i: (i, 0)),
        out_shape=jax.ShapeDtypeStruct((n, PROJ_WIDTH), BF16),
        compiler_params=_cparams("arbitrary"),
        name="in_proj",
    )(x2d, g1.reshape(1, d), w_in_b, cos, sin)


def _diff_attn_kernel(lq1_ref, lk1_ref, lq2_ref, lk2_ref, g_ref, q_ref, k_ref, v_ref, o_ref,
                      qq_sc, m_sc, l_sc, acc_sc, *, lam_init):
    tq, tk = ATT_TQ, ATT_TK
    i = pl.program_id(2)
    q = q_ref[...]
    lane = lax.broadcasted_iota(I32, q.shape, 1)
    zero = jnp.zeros_like(q)
    qq_sc[:tq, :] = jnp.where(lane < DIFF_QK_DIM, q, zero)
    qq_sc[tq:, :] = jnp.where(lane >= DIFF_QK_DIM, q, zero)
    m_sc[...] = jnp.full(m_sc.shape, NEG_BIG, F32)
    l_sc[...] = jnp.zeros(l_sc.shape, F32)
    acc_sc[...] = jnp.zeros(acc_sc.shape, F32)

    def tile(rows, start, width, key_off):
        k = k_ref[pl.ds(start, width), :]
        v = v_ref[pl.ds(start, width), :]
        s = lax.dot_general(qq_sc[rows, :], k, (((1,), (1,)), ((), ())), preferred_element_type=F32)
        if key_off is not None:
            row = (lax.broadcasted_iota(I32, s.shape, 0) + rows.start) & (tq - 1)
            col = lax.broadcasted_iota(I32, s.shape, 1) + key_off
            s = jnp.where(col <= row, s, NEG_BIG)
        m_prev = m_sc[rows, :]
        m_next = jnp.maximum(m_prev, jnp.max(s, axis=1, keepdims=True))
        alpha = jnp.exp2(m_prev - m_next)
        p = jnp.exp2(s - jnp.concatenate([m_next] * (width // LANES), axis=1))
        psum = p[:, :LANES]
        for c in range(1, width // LANES):
            psum = psum + p[:, c * LANES:(c + 1) * LANES]
        l_sc[rows, :] = alpha * l_sc[rows, :] + psum
        acc_sc[rows, :] = alpha * acc_sc[rows, :] + jnp.dot(p.astype(BF16), v, preferred_element_type=F32)
        m_sc[rows, :] = m_next

    def body(j, carry):
        for part in range(2):
            tile(slice(part * tq, (part + 1) * tq), pl.multiple_of(j * tk, tk), tk, None)
        return carry

    lax.fori_loop(0, i, body, 0)
    hk = tk // 2
    diag = pl.multiple_of(i * tk, tk)
    for part in range(2):
        tile(slice(part * tq, (part + 1) * tq), diag, hk, 0)
        tile(slice(part * tq + tq // 2, (part + 1) * tq), diag + hk, hk, hk)

    lam = (jnp.exp(jnp.sum(lq1_ref[...] * lk1_ref[...], axis=-1, keepdims=True))
           - jnp.exp(jnp.sum(lq2_ref[...] * lk2_ref[...], axis=-1, keepdims=True))
           + lam_init)
    o = acc_sc[...] / jnp.sum(l_sc[...], axis=1, keepdims=True)
    d = o[:tq, :] - lam * o[tq:, :]
    ms = jnp.mean(d * d, axis=-1, keepdims=True)
    out = d * lax.rsqrt(ms + RMS_EPS) * g_ref[...] * (1.0 - lam_init)
    o_ref[...] = out.astype(BF16)


def _diff_attention(proj, lq1, lk1, lq2, lk2, g, batch, seq, lam_init):
    n = batch * seq
    tq = ATT_TQ
    assert ATT_TQ == ATT_TK and seq % tq == 0 and DIFF_V_DIM == LANES
    nq = seq // tq
    qcol = COL_DQ // LANES
    kcol = COL_DK // LANES
    vcol = COL_DV // LANES
    vec = lambda b, h, i: (0, 0)
    return pl.pallas_call(
        functools.partial(_diff_attn_kernel, lam_init=lam_init),
        grid=(batch, N_DIFF_HEADS, nq),
        in_specs=[pl.BlockSpec((1, DIFF_QK_DIM), vec)] * 4 + [
            pl.BlockSpec((1, DIFF_V_DIM), vec),
            pl.BlockSpec((tq, LANES), lambda b, h, i: (b * nq + i, qcol + h)),
            pl.BlockSpec((seq, LANES), lambda b, h, i: (b, kcol + h)),
            pl.BlockSpec((seq, LANES), lambda b, h, i: (b, vcol + h))],
        out_specs=pl.BlockSpec((tq, LANES), lambda b, h, i: (b * nq + i, h)),
        out_shape=jax.ShapeDtypeStruct((n, DIFF_WIDTH), BF16),
        scratch_shapes=[pltpu.VMEM((2 * tq, LANES), BF16),
                        pltpu.VMEM((2 * tq, LANES), F32),
                        pltpu.VMEM((2 * tq, LANES), F32),
                        pltpu.VMEM((2 * tq, DIFF_V_DIM), F32)],
        compiler_params=_cparams("arbitrary", "arbitrary", "arbitrary"),
        name="diff_attn",
    )(lq1.reshape(1, -1), lk1.reshape(1, -1), lq2.reshape(1, -1), lk2.reshape(1, -1),
      g.reshape(1, -1), proj, proj, proj)


def _retention_kernel(q_ref, k_ref, v_ref, gate_ref, g_ref, o_ref, state_sc):
    c_len = RET_C

    @pl.when(pl.program_id(1) == 0)
    def _():
        state_sc[...] = jnp.zeros(state_sc.shape, F32)

    ii = lax.broadcasted_iota(I32, (c_len, c_len), 0)
    jj = lax.broadcasted_iota(I32, (c_len, c_len), 1)
    rel = (ii - jj).astype(F32)
    lane = lax.broadcasted_iota(I32, (c_len, LANES), 1)
    pos = lax.broadcasted_iota(I32, (c_len, LANES), 0).astype(F32)
    srow = lax.broadcasted_iota(I32, (LANES, LANES), 0)
    for pair in range(N_RET_HEADS // 2):
        ld = (LOG_DECAY[2 * pair], LOG_DECAY[2 * pair + 1])
        ld_lane = jnp.where(lane < RET_QK_DIM, ld[0], ld[1])
        q_decay = jnp.exp(ld_lane * (pos + 1.0))
        k_decay = jnp.exp(ld_lane * (c_len - 1.0 - pos))
        chunk_decay = jnp.where(srow < RET_QK_DIM, math.exp(ld[0] * c_len), math.exp(ld[1] * c_len))
        intra = [jnp.where(rel >= 0, jnp.exp(l * jnp.maximum(rel, 0.0)), 0.0) for l in ld]
        in_head = (lane < RET_QK_DIM, lane >= RET_QK_DIM)
        for c in range(RET_T // c_len):
            rows = slice(c * c_len, (c + 1) * c_len)
            qb = q_ref[rows, pair * LANES:(pair + 1) * LANES]
            kb = k_ref[rows, pair * LANES:(pair + 1) * LANES]
            q = qb.astype(F32)
            state = state_sc[pair]
            state_b = state.astype(BF16)
            kd_t = (kb.astype(F32) * k_decay).T.astype(BF16)
            new_kv = []
            for hh in range(2):
                h = 2 * pair + hh
                qm = jnp.where(in_head[hh], q, 0.0)
                s = lax.dot_general(qm.astype(BF16), kb, (((1,), (1,)), ((), ())),
                                    preferred_element_type=F32) * intra[hh]
                v = v_ref[rows, h * RET_V_DIM:(h + 1) * RET_V_DIM]
                y = (jnp.dot(s.astype(BF16), v, preferred_element_type=F32)
                     + jnp.dot((qm * q_decay).astype(BF16), state_b, preferred_element_type=F32))
                new_kv.append(jnp.dot(kd_t, v, preferred_element_type=F32))
                ms = jnp.mean(y * y, axis=-1, keepdims=True)
                yn = y * lax.rsqrt(ms + RMS_EPS) * g_ref[h:h + 1, :]
                gate = gate_ref[rows, h * RET_V_DIM:(h + 1) * RET_V_DIM].astype(F32)
                o_ref[rows, h * RET_V_DIM:(h + 1) * RET_V_DIM] = (yn * gate).astype(BF16)
            state_sc[pair] = chunk_decay * state + jnp.where(srow < RET_QK_DIM, new_kv[0], new_kv[1])


def _retention(proj, g, batch, seq):
    n = batch * seq
    t = RET_T
    nt = seq // t
    qk_w = N_RET_HEADS * RET_QK_DIM
    return pl.pallas_call(
        _retention_kernel,
        grid=(batch, nt),
        in_specs=[pl.BlockSpec((t, qk_w), lambda b, i: (b * nt + i, COL_RQ // qk_w)),
                  pl.BlockSpec((t, qk_w), lambda b, i: (b * nt + i, COL_RK // qk_w)),
                  pl.BlockSpec((t, RET_WIDTH), lambda b, i: (b * nt + i, COL_RV // RET_WIDTH)),
                  pl.BlockSpec((t, RET_WIDTH), lambda b, i: (b * nt + i, COL_RG // RET_WIDTH)),
                  pl.BlockSpec((N_RET_HEADS, RET_V_DIM), lambda b, i: (0, 0))],
        out_specs=pl.BlockSpec((t, RET_WIDTH), lambda b, i: (b * nt + i, 0)),
        out_shape=jax.ShapeDtypeStruct((n, RET_WIDTH), BF16),
        scratch_shapes=[pltpu.VMEM((N_RET_HEADS // 2, LANES, RET_V_DIM), F32)],
        compiler_params=_cparams("arbitrary", "arbitrary"),
        name="retention",
    )(proj, proj, proj, proj, g)


def _outproj_kernel(d_ref, r_ref, wo_ref, x_ref, g2_ref, wr_ref, br_ref,
                    x1_ref, h2p_ref, idx_ref, gate_ref, cnt_ref):
    acc = (jnp.dot(d_ref[...], wo_ref[:DIFF_WIDTH, :], preferred_element_type=F32)
           + jnp.dot(r_ref[...], wo_ref[DIFF_WIDTH:, :], preferred_element_type=F32))
    x1 = x_ref[...] + acc
    x1_ref[...] = x1
    ms = jnp.mean(x1 * x1, axis=-1, keepdims=True)
    h2 = x1 * lax.rsqrt(ms + RMS_EPS) * g2_ref[...]
    half = h2.shape[1] // 2
    packed = pltpu.pack_elementwise([h2[:, :half], h2[:, half:]], packed_dtype=BF16)
    slabs = half // LANES
    for j in range(slabs):
        h2p_ref[pl.ds(j, packed.shape[0], stride=slabs), :] = packed[:, j * LANES:(j + 1) * LANES]

    nt = (((1,), (1,)), ((), ()))
    wr = wr_ref[...]
    wr_hi = wr.astype(BF16)
    wr_lo = (wr - wr_hi.astype(F32)).astype(BF16)
    h2_hi = h2.astype(BF16)
    h2_lo = (h2 - h2_hi.astype(F32)).astype(BF16)
    logits = (lax.dot_general(wr_hi, h2_hi, nt, preferred_element_type=F32)
              + lax.dot_general(wr_hi, h2_lo, nt, preferred_element_type=F32)
              + lax.dot_general(wr_lo, h2_hi, nt, preferred_element_type=F32)
              + br_ref[...])
    e_iota = lax.broadcasted_iota(I32, logits.shape, 0)
    vals = []
    chosen = jnp.zeros(logits.shape, F32)
    for r in range(TOP_K):
        m = jnp.max(logits, axis=0, keepdims=True)
        ix = jnp.min(jnp.where(logits == m, e_iota, N_EXPERTS), axis=0, keepdims=True)
        vals.append(m)
        idx_ref[r:r + 1, :] = ix
        hit = e_iota == ix
        chosen = chosen + jnp.where(hit, 1.0, 0.0)
        logits = jnp.where(hit, -jnp.inf, logits)

    @pl.when(pl.program_id(0) == 0)
    def _():
        cnt_ref[...] = jnp.zeros(cnt_ref.shape, F32)

    cnt_ref[...] = cnt_ref[...] + jnp.sum(chosen, axis=1, keepdims=True)
    ex = [jnp.exp(v - vals[0]) for v in vals]
    den = ex[0] + ex[1] + ex[2] + ex[3]
    for r in range(TOP_K):
        gate_ref[r:r + 1, :] = ex[r] / den


def _out_proj(d_out, r_out, w_o_b, x2d, g2, w_router_t, b_router):
    n, d = x2d.shape
    tm = OUT_TM
    const = lambda i: (0, 0)
    return pl.pallas_call(
        _outproj_kernel,
        grid=(n // tm,),
        in_specs=[pl.BlockSpec((tm, DIFF_WIDTH), lambda i: (i, 0)),
                  pl.BlockSpec((tm, RET_WIDTH), lambda i: (i, 0)),
                  pl.BlockSpec((DIFF_WIDTH + RET_WIDTH, d), const),
                  pl.BlockSpec((tm, d), lambda i: (i, 0)),
                  pl.BlockSpec((1, d), const),
                  pl.BlockSpec((N_EXPERTS, d), const),
                  pl.BlockSpec((N_EXPERTS, 1), const)],
        out_specs=[pl.BlockSpec((tm, d), lambda i: (i, 0)),
                   pl.BlockSpec((tm * (d // 2 // LANES), LANES), lambda i: (i, 0)),
                   pl.BlockSpec((TOP_K, tm), lambda i: (0, i)),
                   pl.BlockSpec((TOP_K, tm), lambda i: (0, i)),
                   pl.BlockSpec((N_EXPERTS, 1), const)],
        out_shape=[jax.ShapeDtypeStruct((n, d), F32),
                   jax.ShapeDtypeStruct((n * (d // 2 // LANES), LANES), U32),
                   jax.ShapeDtypeStruct((TOP_K, n), I32),
                   jax.ShapeDtypeStruct((TOP_K, n), F32),
                   jax.ShapeDtypeStruct((N_EXPERTS, 1), F32)],
        compiler_params=_cparams("arbitrary"),
        name="out_proj",
    )(d_out, r_out, w_o_b, x2d, g2.reshape(1, d), w_router_t, b_router.reshape(N_EXPERTS, 1))


def _plan_kernel(idx_ref, cnt_ref, dest_ref, blk_ref, ends_ref, base_sc, tri_sc, *, nb_pad):
    c = pl.program_id(0)
    tc = PLAN_TC
    e_iota = lax.broadcasted_iota(I32, (N_EXPERTS, tc), 0)

    @pl.when(c == 0)
    def _():
        s = lax.broadcasted_iota(I32, (tc, tc), 0)
        t = lax.broadcasted_iota(I32, (tc, tc), 1)
        tri_sc[...] = jnp.where(s < t, 1.0, 0.0).astype(BF16)
        cnt = cnt_ref[...]
        nblk = jnp.floor((cnt + (MOE_BM - 1.0)) * (1.0 / MOE_BM))
        ei = lax.broadcasted_iota(I32, (N_EXPERTS, LANES), 0)
        li = lax.broadcasted_iota(I32, (N_EXPERTS, LANES), 1)
        nblk_row = jnp.sum(jnp.where(ei == li, nblk, 0.0), axis=0, keepdims=True)
        start = jnp.sum(jnp.where(li < ei, nblk_row, 0.0), axis=1, keepdims=True)
        base_sc[...] = start * MOE_BM
        end = start + nblk
        bi = lax.broadcasted_iota(I32, (N_EXPERTS, nb_pad), 1).astype(F32)
        be = jnp.sum(jnp.where(end <= bi, 1.0, 0.0), axis=0, keepdims=True)
        blk_ref[...] = jnp.minimum(be, N_EXPERTS - 1.0).astype(I32)
        ends_ref[...] = jnp.sum(jnp.where(ei == li, end, 0.0), axis=0, keepdims=True).astype(I32)

    base = base_sc[...]
    for k in range(TOP_K):
        oh = idx_ref[k:k + 1, :] == e_iota
        ohb = jnp.where(oh, 1.0, 0.0).astype(BF16)
        before = jnp.dot(ohb, tri_sc[...], preferred_element_type=F32)
        rank = jnp.sum(jnp.where(oh, before + base, 0.0), axis=0, keepdims=True)
        dest_ref[k:k + 1, :] = rank.astype(I32)
        base = base + jnp.sum(jnp.where(oh, 1.0, 0.0), axis=1, keepdims=True)
    base_sc[...] = base


def _plan(top_idx, counts, nb_pad):
    n = top_idx.shape[1]
    tc = PLAN_TC
    return pl.pallas_call(
        functools.partial(_plan_kernel, nb_pad=nb_pad),
        grid=(n // tc,),
        in_specs=[pl.BlockSpec((TOP_K, tc), lambda c: (0, c)),
                  pl.BlockSpec((N_EXPERTS, 1), lambda c: (0, 0))],
        out_specs=[pl.BlockSpec((TOP_K, tc), lambda c: (0, c)),
                   pl.BlockSpec((1, nb_pad), lambda c: (0, 0)),
                   pl.BlockSpec((1, LANES), lambda c: (0, 0))],
        out_shape=[jax.ShapeDtypeStruct((TOP_K, n), I32),
                   jax.ShapeDtypeStruct((1, nb_pad), I32),
                   jax.ShapeDtypeStruct((1, LANES), I32)],
        scratch_shapes=[pltpu.VMEM((N_EXPERTS, 1), F32),
                        pltpu.VMEM((tc, tc), BF16)],
        compiler_params=_cparams("arbitrary"),
        name="plan",
    )(top_idx, counts)


def _dispatch_kernel(ends_ref, dest_ref, h_ref, xbuf_ref, zbuf, sem, zsem, *, nb):
    tm = DISP_TM
    bm = MOE_BM

    @pl.when(pl.program_id(0) == 0)
    def _():
        zbuf[...] = jnp.zeros(zbuf.shape, zbuf.dtype)

        def zero_block(blk):
            return pltpu.make_async_copy(zbuf, xbuf_ref.at[pl.ds(pl.multiple_of(blk * bm, bm), bm)], zsem)

        def per_block(fn):
            for e in range(N_EXPERTS):
                end = ends_ref[e]
                first = ends_ref[e - 1] if e else 0

                @pl.when(end > first)
                def _():
                    fn(zero_block(end - 1))

            def tail(blk, carry):
                fn(zero_block(blk))
                return carry

            lax.fori_loop(ends_ref[N_EXPERTS - 1], nb, tail, 0)

        per_block(lambda cp: cp.start())
        per_block(lambda cp: cp.wait())

    def issue(r, carry):
        for u in range(TOK_PER_ROW):
            for k in range(TOP_K):
                src = h_ref.at[r * TOK_PER_ROW + u]
                dst = xbuf_ref.at[dest_ref[r, u * TOP_K + k]]
                pltpu.make_async_copy(src, dst, sem).start(priority=k % 2)
        return carry

    lax.fori_loop(0, tm // TOK_PER_ROW, issue, 0)
    for k in range(TOP_K):
        pltpu.make_async_copy(xbuf_ref.at[pl.ds(0, tm)], xbuf_ref.at[pl.ds(0, tm)], sem).wait()


def _dispatch(ends, dest, h2p, p_rows):
    n, s, w = h2p.shape
    tm = DISP_TM
    grid_spec = pltpu.PrefetchScalarGridSpec(
        num_scalar_prefetch=1,
        grid=(n // tm,),
        in_specs=[pl.BlockSpec((tm // TOK_PER_ROW, LANES), lambda i, ends: (i, 0), memory_space=pltpu.SMEM),
                  pl.BlockSpec((tm, s, w), lambda i, ends: (i, 0, 0))],
        out_specs=pl.BlockSpec(memory_space=pl.ANY),
        scratch_shapes=[pltpu.VMEM((MOE_BM, s, w), h2p.dtype),
                        pltpu.SemaphoreType.DMA(()),
                        pltpu.SemaphoreType.DMA(())],
    )
    return pl.pallas_call(
        functools.partial(_dispatch_kernel, nb=p_rows // MOE_BM),
        grid_spec=grid_spec,
        out_shape=jax.ShapeDtypeStruct((p_rows, s, w), h2p.dtype),
        compiler_params=pltpu.CompilerParams(dimension_semantics=("arbitrary",),
                                             vmem_limit_bytes=VMEM_LIMIT,
                                             has_side_effects=True),
        name="dispatch",
    )(ends, dest, h2p)


def _pair_perm():
    a = lax.broadcasted_iota(I32, (MXU_DIM, MXU_DIM), 0)
    b = lax.broadcasted_iota(I32, (MXU_DIM, MXU_DIM), 1)
    src = jnp.where(b < LANES, 2 * b, 2 * (b - LANES) + 1)
    return jnp.where(a == src, 1.0, 0.0).astype(BF16)


def _pair_group(b):
    e, f2 = b.shape
    return b.reshape(e, f2 // MXU_DIM, LANES, 2).transpose(0, 1, 3, 2).reshape(e, 1, f2)


def _expert_kernel(blk_ref, ends_ref, x_ref, w1_hbm, b1_ref, w2_hbm, b2_ref, y_ref,
                   w1f, w2f, w1_ref, w2_ref, wsem, slot_ref):
    b = pl.program_id(0)
    n_used = ends_ref[N_EXPERTS - 1]

    @pl.when(b >= n_used)
    def _():
        zero = jnp.zeros(y_ref.shape, F32)
        y_ref[...] = pltpu.pack_elementwise([zero, zero], packed_dtype=BF16)

    def fetch(expert, slot):
        return (pltpu.make_async_copy(w1_hbm.at[expert], w1f.at[slot], wsem.at[slot]),
                pltpu.make_async_copy(w2_hbm.at[expert], w2f.at[slot], wsem.at[slot]))

    @pl.when(b == 0)
    def _():
        slot_ref[0] = 0
        for cp in fetch(blk_ref[0], 0):
            cp.start()

    new_expert = (b == 0) | (blk_ref[b] != blk_ref[jnp.maximum(b - 1, 0)])

    @pl.when((b < n_used) & new_expert)
    def _():
        slot = slot_ref[0]
        expert = blk_ref[b]
        for cp in fetch(expert, slot):
            cp.wait()
        next_first = ends_ref[expert]

        @pl.when(next_first < n_used)
        def _():
            for cp in fetch(blk_ref[next_first], 1 - slot):
                cp.start()

        perm = _pair_perm()
        for c in range(w1f.shape[2] // MXU_DIM):
            cols = slice(c * MXU_DIM, (c + 1) * MXU_DIM)
            blk = w1f[slot, :, cols].astype(BF16)
            w1_ref[0, :, cols] = jnp.dot(blk, perm, preferred_element_type=F32).astype(BF16)
        w2_ref[0] = w2f[slot].astype(BF16)
        slot_ref[0] = 1 - slot

    @pl.when(b < n_used)
    def _():
        ns = w2_ref.shape[2] // 2 // LANES
        bm = x_ref.shape[0] // ns
        slabs = [x_ref[pl.ds(j, bm, stride=ns), :] for j in range(ns)]
        lo = [pltpu.unpack_elementwise(w, index=0, packed_dtype=BF16, unpacked_dtype=F32).astype(BF16)
              for w in slabs]
        hi = [pltpu.unpack_elementwise(w, index=1, packed_dtype=BF16, unpacked_dtype=F32).astype(BF16)
              for w in slabs]
        x = jnp.concatenate(lo + hi, axis=1)
        acc = jnp.zeros((x.shape[0], w2_ref.shape[2]), F32)
        grp = 2 * MXU_DIM
        for c in range(w1_ref.shape[2] // grp):
            cols = slice(c * grp, (c + 1) * grp)
            h = jnp.dot(x, w1_ref[0, :, cols], preferred_element_type=F32) + b1_ref[0, :, cols]
            glu = jnp.concatenate([h[:, 0:LANES], h[:, 2 * LANES:3 * LANES]], axis=1)
            lin = jnp.concatenate([h[:, LANES:2 * LANES], h[:, 3 * LANES:4 * LANES]], axis=1)
            glu = jnp.minimum(glu, SWIGLU_LIMIT)
            lin = jnp.clip(lin, -SWIGLU_LIMIT, SWIGLU_LIMIT)
            act = glu * jax.nn.sigmoid(SWIGLU_ALPHA * glu) * (lin + 1.0)
            acc = acc + jnp.dot(act.astype(BF16), w2_ref[0, c * MXU_DIM:(c + 1) * MXU_DIM, :],
                                preferred_element_type=F32)
        y = acc + b2_ref[0]
        half = y.shape[1] // 2
        packed = pltpu.pack_elementwise([y[:, :half], y[:, half:]], packed_dtype=BF16)
        for j in range(ns):
            y_ref[pl.ds(j, bm, stride=ns), :] = packed[:, j * LANES:(j + 1) * LANES]


def _experts(blk_expert, ends, x_buf, w1, b1, w2, b2):
    e, d, f2 = w1.shape
    xs = d // 2 // LANES
    ys = xs
    p_rows = x_buf.shape[0] // xs
    f = f2 // 2
    bm = MOE_BM
    nb = p_rows // bm

    def row_blk(b, ends):
        return jnp.minimum(b, ends[N_EXPERTS - 1] - 1)

    grid_spec = pltpu.PrefetchScalarGridSpec(
        num_scalar_prefetch=2,
        grid=(nb,),
        in_specs=[pl.BlockSpec((bm * xs, LANES), lambda b, blk, ends: (row_blk(b, ends), 0)),
                  pl.BlockSpec(memory_space=pl.ANY),
                  pl.BlockSpec((1, 1, f2), lambda b, blk, ends: (blk[row_blk(b, ends)], 0, 0)),
                  pl.BlockSpec(memory_space=pl.ANY),
                  pl.BlockSpec((1, 1, d), lambda b, blk, ends: (blk[row_blk(b, ends)], 0, 0))],
        out_specs=pl.BlockSpec((bm * ys, LANES), lambda b, blk, ends: (b, 0)),
        scratch_shapes=[pltpu.VMEM((2, d, f2), F32),
                        pltpu.VMEM((2, f, d), F32),
                        pltpu.VMEM((1, d, f2), BF16),
                        pltpu.VMEM((1, f, d), BF16),
                        pltpu.SemaphoreType.DMA((2,)),
                        pltpu.SMEM((1,), I32)],
    )
    return pl.pallas_call(
        _expert_kernel,
        grid_spec=grid_spec,
        out_shape=jax.ShapeDtypeStruct((p_rows * ys, LANES), x_buf.dtype),
        compiler_params=_cparams("arbitrary"),
        name="experts",
    )(blk_expert, ends, x_buf, w1, b1, w2, b2)


def _combine_kernel(dest_ref, next_dest_ref, gate_ref, x1_ref, gf_ref, ybuf_ref, o_ref, gbuf, sem):
    tm = COMB_TM
    i = pl.program_id(0)
    cur = i % 2

    ys = ybuf_ref.shape[1]

    def gather(idx_ref, buf):
        def issue(r, carry):
            for u in range(TOK_PER_ROW):
                for k in range(TOP_K):
                    src = ybuf_ref.at[idx_ref[r, u * TOP_K + k]]
                    row = pl.multiple_of(r * (TOK_PER_ROW * ys) + u * ys, ys)
                    dst = gbuf.at[buf, k, pl.ds(row, ys), :]
                    pltpu.make_async_copy(src, dst, sem.at[buf]).start(priority=k % 2)
            return carry

        lax.fori_loop(0, tm // TOK_PER_ROW, issue, 0)

    @pl.when(i == 0)
    def _():
        gather(dest_ref, 0)

    @pl.when(i + 1 < pl.num_programs(0))
    def _():
        gather(next_dest_ref, 1 - cur)

    for k in range(TOP_K):
        pltpu.make_async_copy(ybuf_ref.at[pl.ds(0, tm)], ybuf_ref.at[pl.ds(0, tm)], sem.at[cur]).wait()

    gates = gate_ref[...]
    pad = jnp.zeros((LANES - TOP_K, LANES), F32)
    cols = []
    for c in range(tm // LANES):
        blk = jnp.concatenate([gates[:, c * LANES:(c + 1) * LANES], pad], axis=0)
        cols.append(blk.T)
    gcol = jnp.concatenate(cols, axis=0)
    for c in range(tm // COMB_CH):
        rows = slice(c * COMB_CH, (c + 1) * COMB_CH)
        gk = [gcol[rows, k:k + 1] for k in range(TOP_K)]
        sq = jnp.zeros((COMB_CH, LANES), F32)
        for j in range(ys):
            words = [gbuf[cur, k, pl.ds(c * COMB_CH * ys + j, COMB_CH, stride=ys), :] for k in range(TOP_K)]
            for part in range(2):
                col = (part * ys + j) * LANES
                a = x1_ref[rows, col:col + LANES]
                for k in range(TOP_K):
                    yk = pltpu.unpack_elementwise(words[k], index=part, packed_dtype=BF16, unpacked_dtype=F32)
                    a = a + yk * gk[k]
                sq = sq + a * a
                o_ref[rows, col:col + LANES] = a
        ms = jnp.sum(sq, axis=-1, keepdims=True) * (1.0 / o_ref.shape[1])
        o_ref[rows, :] = o_ref[rows, :] * lax.rsqrt(ms + RMS_EPS) * gf_ref[...]


def _combine(dest, gates, x1, gf, y_buf):
    n, d = x1.shape
    tm = COMB_TM
    last = n // tm - 1
    idx_rows = tm // TOK_PER_ROW
    return pl.pallas_call(
        _combine_kernel,
        grid=(n // tm,),
        in_specs=[pl.BlockSpec((idx_rows, LANES), lambda i: (i, 0), memory_space=pltpu.SMEM),
                  pl.BlockSpec((idx_rows, LANES), lambda i: (jnp.minimum(i + 1, last), 0), memory_space=pltpu.SMEM),
                  pl.BlockSpec((TOP_K, tm), lambda i: (0, i)),
                  pl.BlockSpec((tm, d), lambda i: (i, 0)),
                  pl.BlockSpec((1, d), lambda i: (0, 0)),
                  pl.BlockSpec(memory_space=pl.ANY)],
        out_specs=pl.BlockSpec((tm, d), lambda i: (i, 0)),
        out_shape=jax.ShapeDtypeStruct((n, d), F32),
        scratch_shapes=[pltpu.VMEM((2, TOP_K, tm * y_buf.shape[1], LANES), y_buf.dtype),
                        pltpu.SemaphoreType.DMA((2,))],
        compiler_params=_cparams("arbitrary"),
        name="combine",
    )(dest, dest, gates, x1, gf.reshape(1, d), y_buf)


def kernel(x, positions, norm1_g, w_in, lambda_q1, lambda_k1, lambda_q2, lambda_k2, diff_norm_g, ret_norm_g, w_o, norm2_g, w_router, b_router, w_moe_in, b_moe_in, w_moe_out, b_moe_out, norm_f_g):
    batch, seq, d = x.shape
    n = batch * seq
    assert norm1_g.shape[0] == 1, "single-layer block"
    l = 0
    p_rows = n * TOP_K + N_EXPERTS * MOE_BM
    nb = p_rows // MOE_BM
    nb_pad = -(-nb // LANES) * LANES

    cos, sin = _rope_tables(positions)
    x2d = x.reshape(n, d)
    lam_init = 0.8 - 0.6 * math.exp(-0.3 * l)
    proj = _in_proj(x2d, norm1_g[l], w_in[l].astype(BF16), cos, sin)
    d_out = _diff_attention(proj, lambda_q1[l], lambda_k1[l], lambda_q2[l], lambda_k2[l],
                            diff_norm_g[l], batch, seq, lam_init)
    r_out = _retention(proj, ret_norm_g[l], batch, seq)
    x1, h2p, top_idx, gates, counts = _out_proj(d_out, r_out, w_o[l].astype(BF16), x2d, norm2_g[l],
                                                w_router[l].T, b_router[l])
    dest, blk_expert, ends = _plan(top_idx, counts, nb_pad)
    ends = ends.reshape(-1)
    dest = dest.T.reshape(n // TOK_PER_ROW, LANES)
    xs = ys = d // 2 // LANES
    x_buf = _dispatch(ends, dest, h2p.reshape(n, xs, LANES), p_rows)
    b1 = _pair_group(b_moe_in[l])
    b2 = b_moe_out[l].reshape(N_EXPERTS, 1, -1)
    y_buf = _experts(blk_expert.reshape(-1), ends, x_buf.reshape(p_rows * xs, LANES),
                     w_moe_in[l], b1, w_moe_out[l], b2)
    out = _combine(dest, gates, x1, norm_f_g, y_buf.reshape(p_rows, ys, LANES))
    return out.reshape(batch, seq, d)
```

```python
import functools
import math

import numpy as np
import jax
import jax.numpy as jnp
from jax import lax
from jax.experimental import pallas as pl
from jax.experimental.pallas import tpu as pltpu

F32 = jnp.float32
BF16 = jnp.bfloat16
I32 = jnp.int32
U32 = jnp.uint32

N_DIFF_HEADS = 4
DIFF_QK_DIM = 64
DIFF_V_DIM = 128
DIFF_WIDTH = N_DIFF_HEADS * DIFF_V_DIM
N_RET_HEADS = 4
RET_QK_DIM = 64
RET_V_DIM = 128
RET_WIDTH = N_RET_HEADS * RET_V_DIM
ROPE_THETA = 10000.0
RMS_EPS = 1e-5
N_EXPERTS = 32
TOP_K = 4
SWIGLU_LIMIT = 7.0
SWIGLU_ALPHA = 1.702
TOK_PER_ROW = 128 // TOP_K
COL_DQ = 0
COL_DK = 512
COL_DV = 1024
COL_RQ = 1536
COL_RK = 1792
COL_RV = 2048
COL_RG = 2560
PROJ_WIDTH = 3072
LOG_DECAY = tuple(math.log(1.0 - 2.0 ** (-5.0 - h)) for h in range(N_RET_HEADS))
LOG2E = 1.4426950408889634

LANES = 128
SUBLANES = 8
MXU_DIM = 256
VMEM_LIMIT = 56 * 1024 * 1024

PROJ_TM = 512
ATT_TQ = 1024
ATT_TK = 1024
RET_T = 512
RET_C = 256
OUT_TM = 1024
PLAN_TC = 512
DISP_TM = 1024
MOE_BM = 512
COMB_TM = 512
COMB_CH = 64

NEG_BIG = -1e30


def _cparams(*sem):
    return pltpu.CompilerParams(dimension_semantics=sem, vmem_limit_bytes=VMEM_LIMIT)


def _rope_tables(pos4, invf):
    nf = DIFF_QK_DIM // 2
    groups = LANES // nf
    lane = lax.broadcasted_iota(I32, (LANES, LANES), 1)
    first_half = (lane & nf) == 0
    stacked = jnp.concatenate(
        [jnp.broadcast_to(pos4[g:g + 1, :].astype(F32), (nf, LANES)) for g in range(groups)], axis=0)
    ang = stacked.T * invf
    c4 = jnp.cos(ang)
    s4 = jnp.sin(ang)
    cos, sin = [], []
    for g in range(groups):
        mine = (lane // nf) == g

        def spread(t):
            z = jnp.where(mine, t, 0.0)
            return z + pltpu.roll(z, nf, 1) + pltpu.roll(z, 2 * nf, 1) + pltpu.roll(z, 3 * nf, 1)

        s = spread(s4)
        cos.append(spread(c4))
        sin.append(jnp.where(first_half, -s, s))
    return jnp.concatenate(cos, axis=0), jnp.concatenate(sin, axis=0)


def _inproj_kernel(x_ref, g_ref, w_ref, pos_ref, invf_ref, o_ref):
    x = x_ref[...]
    ms = jnp.mean(x * x, axis=-1, keepdims=True)
    h = (x * lax.rsqrt(ms + RMS_EPS) * g_ref[...]).astype(BF16)
    groups = PROJ_TM // LANES
    r0 = pl.multiple_of((pl.program_id(0) % (SUBLANES // groups)) * groups, groups)
    cos, sin = _rope_tables(pos_ref[pl.ds(r0, groups), :], invf_ref[...])
    lane = lax.broadcasted_iota(I32, cos.shape, 1)
    first_half = (lane & 32) == 0

    def rope(t):
        rot = jnp.where(first_half, pltpu.roll(t, 96, 1), pltpu.roll(t, 32, 1))
        return t * cos + rot * sin

    q_scale = DIFF_QK_DIM ** -0.5 * LOG2E
    k_scale = RET_QK_DIM ** -0.5
    for c in range(PROJ_WIDTH // MXU_DIM):
        p = jnp.dot(h, w_ref[:, c * MXU_DIM:(c + 1) * MXU_DIM], preferred_element_type=F32)
        for half in range(MXU_DIM // LANES):
            col = c * MXU_DIM + half * LANES
            t = p[:, half * LANES:(half + 1) * LANES]
            if col < COL_DK:
                t = rope(t) * q_scale
            elif col < COL_DV:
                t = rope(t)
            elif col < COL_RQ:
                pass
            elif col < COL_RK:
                t = rope(t)
            elif col < COL_RV:
                t = rope(t) * k_scale
            elif col < COL_RG:
                pass
            else:
                t = t * jax.nn.sigmoid(t)
            o_ref[:, col:col + LANES] = t.astype(BF16)


def _in_proj(x2d, g1, w_in_b, positions):
    n, d = x2d.shape
    tm = PROJ_TM
    assert tm == (LANES // (DIFF_QK_DIM // 2)) * LANES and DIFF_QK_DIM == RET_QK_DIM
    pos2d = positions.reshape(n // LANES, LANES)
    inv_freq = 1.0 / (ROPE_THETA ** (jnp.arange(0, DIFF_QK_DIM, 2, dtype=F32) / DIFF_QK_DIM))
    invf = jnp.tile(inv_freq, LANES // (DIFF_QK_DIM // 2)).reshape(1, LANES)
    steps_per_pos_block = SUBLANES * LANES // tm
    return pl.pallas_call(
        _inproj_kernel,
        grid=(n // tm,),
        in_specs=[pl.BlockSpec((tm, d), lambda i: (i, 0)),
                  pl.BlockSpec((1, d), lambda i: (0, 0)),
                  pl.BlockSpec((d, PROJ_WIDTH), lambda i: (0, 0)),
                  pl.BlockSpec((SUBLANES, LANES), lambda i: (i // steps_per_pos_block, 0)),
                  pl.BlockSpec((1, LANES), lambda i: (0, 0))],
        out_specs=pl.BlockSpec((tm, PROJ_WIDTH), lambda i: (i, 0)),
        out_shape=jax.ShapeDtypeStruct((n, PROJ_WIDTH), BF16),
        compiler_params=_cparams("arbitrary"),
        name="in_proj",
    )(x2d, g1.reshape(1, d), w_in_b, pos2d, invf)


def _diff_attn_kernel(lq1_ref, lk1_ref, lq2_ref, lk2_ref, g_ref, q_ref, k_ref, v_ref, o_ref,
                      qq_sc, m_sc, l_sc, acc_sc, *, lam_init):
    tq, tk = ATT_TQ, ATT_TK
    i = pl.program_id(2)
    q = q_ref[...]
    lane = lax.broadcasted_iota(I32, q.shape, 1)
    zero = jnp.zeros_like(q)
    qq_sc[:tq, :] = jnp.where(lane < DIFF_QK_DIM, q, zero)
    qq_sc[tq:, :] = jnp.where(lane >= DIFF_QK_DIM, q, zero)
    m_sc[...] = jnp.full(m_sc.shape, NEG_BIG, F32)
    l_sc[...] = jnp.zeros(l_sc.shape, F32)
    acc_sc[...] = jnp.zeros(acc_sc.shape, F32)

    def tile(rows, start, width, key_off):
        k = k_ref[pl.ds(start, width), :]
        v = v_ref[pl.ds(start, width), :]
        s = lax.dot_general(qq_sc[rows, :], k, (((1,), (1,)), ((), ())), preferred_element_type=F32)
        if key_off is not None:
            row = (lax.broadcasted_iota(I32, s.shape, 0) + rows.start) & (tq - 1)
            col = lax.broadcasted_iota(I32, s.shape, 1) + key_off
            s = jnp.where(col <= row, s, NEG_BIG)
        m_prev = m_sc[rows, :]
        m_next = jnp.maximum(m_prev, jnp.max(s, axis=1, keepdims=True))
        alpha = jnp.exp2(m_prev - m_next)
        p = jnp.exp2(s - jnp.concatenate([m_next] * (width // LANES), axis=1))
        psum = p[:, :LANES]
        for c in range(1, width // LANES):
            psum = psum + p[:, c * LANES:(c + 1) * LANES]
        l_sc[rows, :] = alpha * l_sc[rows, :] + psum
        acc_sc[rows, :] = alpha * acc_sc[rows, :] + jnp.dot(p.astype(BF16), v, preferred_element_type=F32)
        m_sc[rows, :] = m_next

    def body(j, carry):
        for part in range(2):
            tile(slice(part * tq, (part + 1) * tq), pl.multiple_of(j * tk, tk), tk, None)
        return carry

    lax.fori_loop(0, i, body, 0)
    hk = tk // 2
    diag = pl.multiple_of(i * tk, tk)
    for part in range(2):
        tile(slice(part * tq, (part + 1) * tq), diag, hk, 0)
        tile(slice(part * tq + tq // 2, (part + 1) * tq), diag + hk, hk, hk)

    lam = (jnp.exp(jnp.sum(lq1_ref[...] * lk1_ref[...], axis=-1, keepdims=True))
           - jnp.exp(jnp.sum(lq2_ref[...] * lk2_ref[...], axis=-1, keepdims=True))
           + lam_init)
    o = acc_sc[...] / jnp.sum(l_sc[...], axis=1, keepdims=True)
    d = o[:tq, :] - lam * o[tq:, :]
    ms = jnp.mean(d * d, axis=-1, keepdims=True)
    out = d * lax.rsqrt(ms + RMS_EPS) * g_ref[...] * (1.0 - lam_init)
    o_ref[...] = out.astype(BF16)


def _diff_attention(proj, lq1, lk1, lq2, lk2, g, batch, seq, lam_init):
    n = batch * seq
    tq = ATT_TQ
    assert ATT_TQ == ATT_TK and seq % tq == 0 and DIFF_V_DIM == LANES
    nq = seq // tq
    qcol = COL_DQ // LANES
    kcol = COL_DK // LANES
    vcol = COL_DV // LANES
    vec = lambda b, h, i: (0, 0)
    return pl.pallas_call(
        functools.partial(_diff_attn_kernel, lam_init=lam_init),
        grid=(batch, N_DIFF_HEADS, nq),
        in_specs=[pl.BlockSpec((1, DIFF_QK_DIM), vec)] * 4 + [
            pl.BlockSpec((1, DIFF_V_DIM), vec),
            pl.BlockSpec((tq, LANES), lambda b, h, i: (b * nq + i, qcol + h)),
            pl.BlockSpec((seq, LANES), lambda b, h, i: (b, kcol + h)),
            pl.BlockSpec((seq, LANES), lambda b, h, i: (b, vcol + h))],
        out_specs=pl.BlockSpec((tq, LANES), lambda b, h, i: (b * nq + i, h)),
        out_shape=jax.ShapeDtypeStruct((n, DIFF_WIDTH), BF16),
        scratch_shapes=[pltpu.VMEM((2 * tq, LANES), BF16),
                        pltpu.VMEM((2 * tq, LANES), F32),
                        pltpu.VMEM((2 * tq, LANES), F32),
                        pltpu.VMEM((2 * tq, DIFF_V_DIM), F32)],
        compiler_params=_cparams("arbitrary", "arbitrary", "arbitrary"),
        name="diff_attn",
    )(lq1.reshape(1, -1), lk1.reshape(1, -1), lq2.reshape(1, -1), lk2.reshape(1, -1),
      g.reshape(1, -1), proj, proj, proj)


def _retention_kernel(q_ref, k_ref, v_ref, gate_ref, g_ref, o_ref, state_sc, qdec_sc, kdec_sc, intra_sc):
    c_len = RET_C
    lane = lax.broadcasted_iota(I32, (c_len, LANES), 1)
    srow = lax.broadcasted_iota(I32, (LANES, LANES), 0)

    @pl.when((pl.program_id(0) == 0) & (pl.program_id(1) == 0))
    def _():
        ii = lax.broadcasted_iota(I32, (c_len, c_len), 0)
        jj = lax.broadcasted_iota(I32, (c_len, c_len), 1)
        rel = (ii - jj).astype(F32)
        pos = lax.broadcasted_iota(I32, (c_len, LANES), 0).astype(F32)
        for pair in range(N_RET_HEADS // 2):
            ld = (LOG_DECAY[2 * pair], LOG_DECAY[2 * pair + 1])
            ld_lane = jnp.where(lane < RET_QK_DIM, ld[0], ld[1])
            qdec_sc[pair] = jnp.exp(ld_lane * (pos + 1.0))
            kdec_sc[pair] = jnp.exp(ld_lane * (c_len - 1.0 - pos))
            for hh in range(2):
                intra_sc[2 * pair + hh] = jnp.where(rel >= 0, jnp.exp(ld[hh] * jnp.maximum(rel, 0.0)), 0.0)

    @pl.when(pl.program_id(1) == 0)
    def _():
        state_sc[...] = jnp.zeros(state_sc.shape, F32)

    for pair in range(N_RET_HEADS // 2):
        ld = (LOG_DECAY[2 * pair], LOG_DECAY[2 * pair + 1])
        q_decay = qdec_sc[pair]
        k_decay = kdec_sc[pair]
        chunk_decay = jnp.where(srow < RET_QK_DIM, math.exp(ld[0] * c_len), math.exp(ld[1] * c_len))
        intra = [intra_sc[2 * pair], intra_sc[2 * pair + 1]]
        in_head = (lane < RET_QK_DIM, lane >= RET_QK_DIM)
        for c in range(RET_T // c_len):
            rows = slice(c * c_len, (c + 1) * c_len)
            qb = q_ref[rows, pair * LANES:(pair + 1) * LANES]
            kb = k_ref[rows, pair * LANES:(pair + 1) * LANES]
            q = qb.astype(F32)
            state = state_sc[pair]
            state_b = state.astype(BF16)
            kd_t = (kb.astype(F32) * k_decay).T.astype(BF16)
            new_kv = []
            for hh in range(2):
                h = 2 * pair + hh
                qm = jnp.where(in_head[hh], q, 0.0)
                s = lax.dot_general(qm.astype(BF16), kb, (((1,), (1,)), ((), ())),
                                    preferred_element_type=F32) * intra[hh]
                v = v_ref[rows, h * RET_V_DIM:(h + 1) * RET_V_DIM]
                y = (jnp.dot(s.astype(BF16), v, preferred_element_type=F32)
                     + jnp.dot((qm * q_decay).astype(BF16), state_b, preferred_element_type=F32))
                new_kv.append(jnp.dot(kd_t, v, preferred_element_type=F32))
                ms = jnp.mean(y * y, axis=-1, keepdims=True)
                yn = y * lax.rsqrt(ms + RMS_EPS) * g_ref[h:h + 1, :]
                gate = gate_ref[rows, h * RET_V_DIM:(h + 1) * RET_V_DIM].astype(F32)
                o_ref[rows, h * RET_V_DIM:(h + 1) * RET_V_DIM] = (yn * gate).astype(BF16)
            state_sc[pair] = chunk_decay * state + jnp.where(srow < RET_QK_DIM, new_kv[0], new_kv[1])


def _retention(proj, g, batch, seq):
    n = batch * seq
    t = RET_T
    nt = seq // t
    qk_w = N_RET_HEADS * RET_QK_DIM
    return pl.pallas_call(
        _retention_kernel,
        grid=(batch, nt),
        in_specs=[pl.BlockSpec((t, qk_w), lambda b, i: (b * nt + i, COL_RQ // qk_w)),
                  pl.BlockSpec((t, qk_w), lambda b, i: (b * nt + i, COL_RK // qk_w)),
                  pl.BlockSpec((t, RET_WIDTH), lambda b, i: (b * nt + i, COL_RV // RET_WIDTH)),
                  pl.BlockSpec((t, RET_WIDTH), lambda b, i: (b * nt + i, COL_RG // RET_WIDTH)),
                  pl.BlockSpec((N_RET_HEADS, RET_V_DIM), lambda b, i: (0, 0))],
        out_specs=pl.BlockSpec((t, RET_WIDTH), lambda b, i: (b * nt + i, 0)),
        out_shape=jax.ShapeDtypeStruct((n, RET_WIDTH), BF16),
        scratch_shapes=[pltpu.VMEM((N_RET_HEADS // 2, LANES, RET_V_DIM), F32),
                        pltpu.VMEM((N_RET_HEADS // 2, RET_C, LANES), F32),
                        pltpu.VMEM((N_RET_HEADS // 2, RET_C, LANES), F32),
                        pltpu.VMEM((N_RET_HEADS, RET_C, RET_C), F32)],
        compiler_params=_cparams("arbitrary", "arbitrary"),
        name="retention",
    )(proj, proj, proj, proj, g)


def _outproj_kernel(d_ref, r_ref, wo_ref, x_ref, g2_ref, wr_ref, br_ref,
                    x1_ref, h2p_ref, idx_ref, gate_ref, cnt_ref):
    acc = (jnp.dot(d_ref[...], wo_ref[:DIFF_WIDTH, :], preferred_element_type=F32)
           + jnp.dot(r_ref[...], wo_ref[DIFF_WIDTH:, :], preferred_element_type=F32))
    x1 = x_ref[...] + acc
    x1_ref[...] = x1
    ms = jnp.mean(x1 * x1, axis=-1, keepdims=True)
    h2 = x1 * lax.rsqrt(ms + RMS_EPS) * g2_ref[...]
    half = h2.shape[1] // 2
    packed = pltpu.pack_elementwise([h2[:, :half], h2[:, half:]], packed_dtype=BF16)
    slabs = half // LANES
    for j in range(slabs):
        h2p_ref[pl.ds(j, packed.shape[0], stride=slabs), :] = packed[:, j * LANES:(j + 1) * LANES]

    nt = (((1,), (1,)), ((), ()))
    wr = wr_ref[...]
    wr_hi = wr.astype(BF16)
    wr_lo = (wr - wr_hi.astype(F32)).astype(BF16)
    h2_hi = h2.astype(BF16)
    h2_lo = (h2 - h2_hi.astype(F32)).astype(BF16)
    logits = (lax.dot_general(wr_hi, h2_hi, nt, preferred_element_type=F32)
              + lax.dot_general(wr_hi, h2_lo, nt, preferred_element_type=F32)
              + lax.dot_general(wr_lo, h2_hi, nt, preferred_element_type=F32)
              + br_ref[...])
    e_iota = lax.broadcasted_iota(I32, logits.shape, 0)
    vals = []
    chosen = jnp.zeros(logits.shape, F32)
    for r in range(TOP_K):
        m = jnp.max(logits, axis=0, keepdims=True)
        ix = jnp.min(jnp.where(logits == m, e_iota, N_EXPERTS), axis=0, keepdims=True)
        vals.append(m)
        idx_ref[r:r + 1, :] = ix
        hit = e_iota == ix
        chosen = chosen + jnp.where(hit, 1.0, 0.0)
        logits = jnp.where(hit, -jnp.inf, logits)

    @pl.when(pl.program_id(0) == 0)
    def _():
        cnt_ref[...] = jnp.zeros(cnt_ref.shape, F32)

    cnt_ref[...] = cnt_ref[...] + jnp.sum(chosen, axis=1, keepdims=True)
    ex = [jnp.exp(v - vals[0]) for v in vals]
    den = ex[0] + ex[1] + ex[2] + ex[3]
    for r in range(TOP_K):
        gate_ref[r:r + 1, :] = ex[r] / den


def _out_proj(d_out, r_out, w_o_b, x2d, g2, w_router_t, b_router):
    n, d = x2d.shape
    tm = OUT_TM
    const = lambda i: (0, 0)
    return pl.pallas_call(
        _outproj_kernel,
        grid=(n // tm,),
        in_specs=[pl.BlockSpec((tm, DIFF_WIDTH), lambda i: (i, 0)),
                  pl.BlockSpec((tm, RET_WIDTH), lambda i: (i, 0)),
                  pl.BlockSpec((DIFF_WIDTH + RET_WIDTH, d), const),
                  pl.BlockSpec((tm, d), lambda i: (i, 0)),
                  pl.BlockSpec((1, d), const),
                  pl.BlockSpec((N_EXPERTS, d), const),
                  pl.BlockSpec((N_EXPERTS, 1), const)],
        out_specs=[pl.BlockSpec((tm, d), lambda i: (i, 0)),
                   pl.BlockSpec((tm * (d // 2 // LANES), LANES), lambda i: (i, 0)),
                   pl.BlockSpec((TOP_K, tm), lambda i: (0, i)),
                   pl.BlockSpec((TOP_K, tm), lambda i: (0, i)),
                   pl.BlockSpec((N_EXPERTS, 1), const)],
        out_shape=[jax.ShapeDtypeStruct((n, d), F32),
                   jax.ShapeDtypeStruct((n * (d // 2 // LANES), LANES), U32),
                   jax.ShapeDtypeStruct((TOP_K, n), I32),
                   jax.ShapeDtypeStruct((TOP_K, n), F32),
                   jax.ShapeDtypeStruct((N_EXPERTS, 1), F32)],
        compiler_params=_cparams("arbitrary"),
        name="out_proj",
    )(d_out, r_out, w_o_b, x2d, g2.reshape(1, d), w_router_t, b_router.reshape(N_EXPERTS, 1))


def _plan_kernel(idx_ref, cnt_ref, dest_ref, blk_ref, ends_ref, base_sc, tri_sc, *, nb_pad):
    c = pl.program_id(0)
    tc = PLAN_TC
    e_iota = lax.broadcasted_iota(I32, (N_EXPERTS, tc), 0)

    @pl.when(c == 0)
    def _():
        s = lax.broadcasted_iota(I32, (tc, tc), 0)
        t = lax.broadcasted_iota(I32, (tc, tc), 1)
        tri_sc[...] = jnp.where(s < t, 1.0, 0.0).astype(BF16)
        cnt = cnt_ref[...]
        nblk = jnp.floor((cnt + (MOE_BM - 1.0)) * (1.0 / MOE_BM))
        ei = lax.broadcasted_iota(I32, (N_EXPERTS, LANES), 0)
        li = lax.broadcasted_iota(I32, (N_EXPERTS, LANES), 1)
        nblk_row = jnp.sum(jnp.where(ei == li, nblk, 0.0), axis=0, keepdims=True)
        start = jnp.sum(jnp.where(li < ei, nblk_row, 0.0), axis=1, keepdims=True)
        base_sc[...] = start * MOE_BM
        end = start + nblk
        bi = lax.broadcasted_iota(I32, (N_EXPERTS, nb_pad), 1).astype(F32)
        be = jnp.sum(jnp.where(end <= bi, 1.0, 0.0), axis=0, keepdims=True)
        blk_ref[...] = jnp.minimum(be, N_EXPERTS - 1.0).astype(I32)
        ends_ref[...] = jnp.sum(jnp.where(ei == li, end, 0.0), axis=0, keepdims=True).astype(I32)

    base = base_sc[...]
    for k in range(TOP_K):
        oh = idx_ref[k:k + 1, :] == e_iota
        ohb = jnp.where(oh, 1.0, 0.0).astype(BF16)
        before = jnp.dot(ohb, tri_sc[...], preferred_element_type=F32)
        rank = jnp.sum(jnp.where(oh, before + base, 0.0), axis=0, keepdims=True)
        dest_ref[k:k + 1, :] = rank.astype(I32)
        base = base + jnp.sum(jnp.where(oh, 1.0, 0.0), axis=1, keepdims=True)
    base_sc[...] = base


def _plan(top_idx, counts, nb_pad):
    n = top_idx.shape[1]
    tc = PLAN_TC
    return pl.pallas_call(
        functools.partial(_plan_kernel, nb_pad=nb_pad),
        grid=(n // tc,),
        in_specs=[pl.BlockSpec((TOP_K, tc), lambda c: (0, c)),
                  pl.BlockSpec((N_EXPERTS, 1), lambda c: (0, 0))],
        out_specs=[pl.BlockSpec((TOP_K, tc), lambda c: (0, c)),
                   pl.BlockSpec((1, nb_pad), lambda c: (0, 0)),
                   pl.BlockSpec((1, LANES), lambda c: (0, 0))],
        out_shape=[jax.ShapeDtypeStruct((TOP_K, n), I32),
                   jax.ShapeDtypeStruct((1, nb_pad), I32),
                   jax.ShapeDtypeStruct((1, LANES), I32)],
        scratch_shapes=[pltpu.VMEM((N_EXPERTS, 1), F32),
                        pltpu.VMEM((tc, tc), BF16)],
        compiler_params=_cparams("arbitrary"),
        name="plan",
    )(top_idx, counts)


def _dispatch_kernel(ends_ref, dest_ref, h_ref, xbuf_ref, zbuf, sem, zsem, *, nb):
    tm = DISP_TM
    bm = MOE_BM

    @pl.when(pl.program_id(0) == 0)
    def _():
        zbuf[...] = jnp.zeros(zbuf.shape, zbuf.dtype)

        def zero_block(blk):
            return pltpu.make_async_copy(zbuf, xbuf_ref.at[pl.ds(pl.multiple_of(blk * bm, bm), bm)], zsem)

        def per_block(fn):
            for e in range(N_EXPERTS):
                end = ends_ref[e]
                first = ends_ref[e - 1] if e else 0

                @pl.when(end > first)
                def _():
                    fn(zero_block(end - 1))

            def tail(blk, carry):
                fn(zero_block(blk))
                return carry

            lax.fori_loop(ends_ref[N_EXPERTS - 1], nb, tail, 0)

        per_block(lambda cp: cp.start())
        per_block(lambda cp: cp.wait())

    def issue(r, carry):
        for u in range(TOK_PER_ROW):
            for k in range(TOP_K):
                src = h_ref.at[r * TOK_PER_ROW + u]
                dst = xbuf_ref.at[dest_ref[r, u * TOP_K + k]]
                pltpu.make_async_copy(src, dst, sem).start(priority=k % 2)
        return carry

    lax.fori_loop(0, tm // TOK_PER_ROW, issue, 0)
    for k in range(TOP_K):
        pltpu.make_async_copy(xbuf_ref.at[pl.ds(0, tm)], xbuf_ref.at[pl.ds(0, tm)], sem).wait()


def _dispatch(ends, dest, h2p, p_rows):
    n, s, w = h2p.shape
    tm = DISP_TM
    grid_spec = pltpu.PrefetchScalarGridSpec(
        num_scalar_prefetch=1,
        grid=(n // tm,),
        in_specs=[pl.BlockSpec((tm // TOK_PER_ROW, LANES), lambda i, ends: (i, 0), memory_space=pltpu.SMEM),
                  pl.BlockSpec((tm, s, w), lambda i, ends: (i, 0, 0))],
        out_specs=pl.BlockSpec(memory_space=pl.ANY),
        scratch_shapes=[pltpu.VMEM((MOE_BM, s, w), h2p.dtype),
                        pltpu.SemaphoreType.DMA(()),
                        pltpu.SemaphoreType.DMA(())],
    )
    return pl.pallas_call(
        functools.partial(_dispatch_kernel, nb=p_rows // MOE_BM),
        grid_spec=grid_spec,
        out_shape=jax.ShapeDtypeStruct((p_rows, s, w), h2p.dtype),
        compiler_params=pltpu.CompilerParams(dimension_semantics=("arbitrary",),
                                             vmem_limit_bytes=VMEM_LIMIT,
                                             has_side_effects=True),
        name="dispatch",
    )(ends, dest, h2p)


def _pair_perm():
    a = lax.broadcasted_iota(I32, (MXU_DIM, MXU_DIM), 0)
    b = lax.broadcasted_iota(I32, (MXU_DIM, MXU_DIM), 1)
    src = jnp.where(b < LANES, 2 * b, 2 * (b - LANES) + 1)
    return jnp.where(a == src, 1.0, 0.0).astype(BF16)


def _pair_group(b):
    e, f2 = b.shape
    return b.reshape(e, f2 // MXU_DIM, LANES, 2).transpose(0, 1, 3, 2).reshape(e, 1, f2)


def _expert_kernel(blk_ref, ends_ref, x_ref, w1_hbm, b1_ref, w2_hbm, b2_ref, y_ref,
                   w1f, w2f, w1_ref, w2_ref, wsem, slot_ref):
    b = pl.program_id(0)
    n_used = ends_ref[N_EXPERTS - 1]

    @pl.when(b >= n_used)
    def _():
        zero = jnp.zeros(y_ref.shape, F32)
        y_ref[...] = pltpu.pack_elementwise([zero, zero], packed_dtype=BF16)

    def fetch(expert, slot):
        return (pltpu.make_async_copy(w1_hbm.at[expert], w1f.at[slot], wsem.at[slot]),
                pltpu.make_async_copy(w2_hbm.at[expert], w2f.at[slot], wsem.at[slot]))

    @pl.when(b == 0)
    def _():
        slot_ref[0] = 0
        for cp in fetch(blk_ref[0], 0):
            cp.start()

    new_expert = (b == 0) | (blk_ref[b] != blk_ref[jnp.maximum(b - 1, 0)])

    @pl.when((b < n_used) & new_expert)
    def _():
        slot = slot_ref[0]
        expert = blk_ref[b]
        for cp in fetch(expert, slot):
            cp.wait()
        next_first = ends_ref[expert]

        @pl.when(next_first < n_used)
        def _():
            for cp in fetch(blk_ref[next_first], 1 - slot):
                cp.start()

        perm = _pair_perm()
        for c in range(w1f.shape[2] // MXU_DIM):
            cols = slice(c * MXU_DIM, (c + 1) * MXU_DIM)
            blk = w1f[slot, :, cols].astype(BF16)
            w1_ref[0, :, cols] = jnp.dot(blk, perm, preferred_element_type=F32).astype(BF16)
        w2_ref[0] = w2f[slot].astype(BF16)
        slot_ref[0] = 1 - slot

    @pl.when(b < n_used)
    def _():
        ns = w2_ref.shape[2] // 2 // LANES
        bm = x_ref.shape[0] // ns
        slabs = [x_ref[pl.ds(j, bm, stride=ns), :] for j in range(ns)]
        lo = [pltpu.unpack_elementwise(w, index=0, packed_dtype=BF16, unpacked_dtype=F32).astype(BF16)
              for w in slabs]
        hi = [pltpu.unpack_elementwise(w, index=1, packed_dtype=BF16, unpacked_dtype=F32).astype(BF16)
              for w in slabs]
        x = jnp.concatenate(lo + hi, axis=1)
        acc = jnp.zeros((x.shape[0], w2_ref.shape[2]), F32)
        grp = 2 * MXU_DIM
        for c in range(w1_ref.shape[2] // grp):
            cols = slice(c * grp, (c + 1) * grp)
            h = jnp.dot(x, w1_ref[0, :, cols], preferred_element_type=F32) + b1_ref[0, :, cols]
            glu = jnp.concatenate([h[:, 0:LANES], h[:, 2 * LANES:3 * LANES]], axis=1)
            lin = jnp.concatenate([h[:, LANES:2 * LANES], h[:, 3 * LANES:4 * LANES]], axis=1)
            glu = jnp.minimum(glu, SWIGLU_LIMIT)
            lin = jnp.clip(lin, -SWIGLU_LIMIT, SWIGLU_LIMIT)
            act = glu * jax.nn.sigmoid(SWIGLU_ALPHA * glu) * (lin + 1.0)
            acc = acc + jnp.dot(act.astype(BF16), w2_ref[0, c * MXU_DIM:(c + 1) * MXU_DIM, :],
                                preferred_element_type=F32)
        y = acc + b2_ref[0]
        half = y.shape[1] // 2
        packed = pltpu.pack_elementwise([y[:, :half], y[:, half:]], packed_dtype=BF16)
        for j in range(ns):
            y_ref[pl.ds(j, bm, stride=ns), :] = packed[:, j * LANES:(j + 1) * LANES]


def _experts(blk_expert, ends, x_buf, w1, b1, w2, b2):
    e, d, f2 = w1.shape
    xs = d // 2 // LANES
    ys = xs
    p_rows = x_buf.shape[0] // xs
    f = f2 // 2
    bm = MOE_BM
    nb = p_rows // bm

    def row_blk(b, ends):
        return jnp.minimum(b, ends[N_EXPERTS - 1] - 1)

    grid_spec = pltpu.PrefetchScalarGridSpec(
        num_scalar_prefetch=2,
        grid=(nb,),
        in_specs=[pl.BlockSpec((bm * xs, LANES), lambda b, blk, ends: (row_blk(b, ends), 0)),
                  pl.BlockSpec(memory_space=pl.ANY),
                  pl.BlockSpec((1, 1, f2), lambda b, blk, ends: (blk[row_blk(b, ends)], 0, 0)),
                  pl.BlockSpec(memory_space=pl.ANY),
                  pl.BlockSpec((1, 1, d), lambda b, blk, ends: (blk[row_blk(b, ends)], 0, 0))],
        out_specs=pl.BlockSpec((bm * ys, LANES), lambda b, blk, ends: (b, 0)),
        scratch_shapes=[pltpu.VMEM((2, d, f2), F32),
                        pltpu.VMEM((2, f, d), F32),
                        pltpu.VMEM((1, d, f2), BF16),
                        pltpu.VMEM((1, f, d), BF16),
                        pltpu.SemaphoreType.DMA((2,)),
                        pltpu.SMEM((1,), I32)],
    )
    return pl.pallas_call(
        _expert_kernel,
        grid_spec=grid_spec,
        out_shape=jax.ShapeDtypeStruct((p_rows * ys, LANES), x_buf.dtype),
        compiler_params=_cparams("arbitrary"),
        name="experts",
    )(blk_expert, ends, x_buf, w1, b1, w2, b2)


def _combine_kernel(dest_ref, next_dest_ref, gate_ref, x1_ref, gf_ref, ybuf_ref, o_ref, gbuf, sem):
    tm = COMB_TM
    i = pl.program_id(0)
    cur = i % 2

    ys = ybuf_ref.shape[1]

    def gather(idx_ref, buf):
        def issue(r, carry):
            for u in range(TOK_PER_ROW):
                for k in range(TOP_K):
                    src = ybuf_ref.at[idx_ref[r, u * TOP_K + k]]
                    row = pl.multiple_of(r * (TOK_PER_ROW * ys) + u * ys, ys)
                    dst = gbuf.at[buf, k, pl.ds(row, ys), :]
                    pltpu.make_async_copy(src, dst, sem.at[buf]).start(priority=k % 2)
            return carry

        lax.fori_loop(0, tm // TOK_PER_ROW, issue, 0)

    @pl.when(i == 0)
    def _():
        gather(dest_ref, 0)

    @pl.when(i + 1 < pl.num_programs(0))
    def _():
        gather(next_dest_ref, 1 - cur)

    for k in range(TOP_K):
        pltpu.make_async_copy(ybuf_ref.at[pl.ds(0, tm)], ybuf_ref.at[pl.ds(0, tm)], sem.at[cur]).wait()

    gates = gate_ref[...]
    pad = jnp.zeros((LANES - TOP_K, LANES), F32)
    cols = []
    for c in range(tm // LANES):
        blk = jnp.concatenate([gates[:, c * LANES:(c + 1) * LANES], pad], axis=0)
        cols.append(blk.T)
    gcol = jnp.concatenate(cols, axis=0)
    for c in range(tm // COMB_CH):
        rows = slice(c * COMB_CH, (c + 1) * COMB_CH)
        gk = [gcol[rows, k:k + 1] for k in range(TOP_K)]
        sq = jnp.zeros((COMB_CH, LANES), F32)
        for j in range(ys):
            words = [gbuf[cur, k, pl.ds(c * COMB_CH * ys + j, COMB_CH, stride=ys), :] for k in range(TOP_K)]
            for part in range(2):
                col = (part * ys + j) * LANES
                a = x1_ref[rows, col:col + LANES]
                for k in range(TOP_K):
                    yk = pltpu.unpack_elementwise(words[k], index=part, packed_dtype=BF16, unpacked_dtype=F32)
                    a = a + yk * gk[k]
                sq = sq + a * a
                o_ref[rows, col:col + LANES] = a
        ms = jnp.sum(sq, axis=-1, keepdims=True) * (1.0 / o_ref.shape[1])
        o_ref[rows, :] = o_ref[rows, :] * lax.rsqrt(ms + RMS_EPS) * gf_ref[...]


def _combine(dest, gates, x1, gf, y_buf):
    n, d = x1.shape
    tm = COMB_TM
    last = n // tm - 1
    idx_rows = tm // TOK_PER_ROW
    return pl.pallas_call(
        _combine_kernel,
        grid=(n // tm,),
        in_specs=[pl.BlockSpec((idx_rows, LANES), lambda i: (i, 0), memory_space=pltpu.SMEM),
                  pl.BlockSpec((idx_rows, LANES), lambda i: (jnp.minimum(i + 1, last), 0), memory_space=pltpu.SMEM),
                  pl.BlockSpec((TOP_K, tm), lambda i: (0, i)),
                  pl.BlockSpec((tm, d), lambda i: (i, 0)),
                  pl.BlockSpec((1, d), lambda i: (0, 0)),
                  pl.BlockSpec(memory_space=pl.ANY)],
        out_specs=pl.BlockSpec((tm, d), lambda i: (i, 0)),
        out_shape=jax.ShapeDtypeStruct((n, d), F32),
        scratch_shapes=[pltpu.VMEM((2, TOP_K, tm * y_buf.shape[1], LANES), y_buf.dtype),
                        pltpu.SemaphoreType.DMA((2,))],
        compiler_params=_cparams("arbitrary"),
        name="combine",
    )(dest, dest, gates, x1, gf.reshape(1, d), y_buf)


def kernel(x, positions, norm1_g, w_in, lambda_q1, lambda_k1, lambda_q2, lambda_k2, diff_norm_g, ret_norm_g, w_o, norm2_g, w_router, b_router, w_moe_in, b_moe_in, w_moe_out, b_moe_out, norm_f_g):
    batch, seq, d = x.shape
    n = batch * seq
    assert norm1_g.shape[0] == 1, "single-layer block"
    l = 0
    p_rows = n * TOP_K + N_EXPERTS * MOE_BM
    nb = p_rows // MOE_BM
    nb_pad = -(-nb // LANES) * LANES

    x2d = x.reshape(n, d)
    lam_init = 0.8 - 0.6 * math.exp(-0.3 * l)
    proj = _in_proj(x2d, norm1_g[l], w_in[l].astype(BF16), positions)
    d_out = _diff_attention(proj, lambda_q1[l], lambda_k1[l], lambda_q2[l], lambda_k2[l],
                            diff_norm_g[l], batch, seq, lam_init)
    r_out = _retention(proj, ret_norm_g[l], batch, seq)
    x1, h2p, top_idx, gates, counts = _out_proj(d_out, r_out, w_o[l].astype(BF16), x2d, norm2_g[l],
                                                w_router[l].T, b_router[l])
    dest, blk_expert, ends = _plan(top_idx, counts, nb_pad)
    ends = ends.reshape(-1)
    dest = dest.T.reshape(n // TOK_PER_ROW, LANES)
    xs = ys = d // 2 // LANES
    x_buf = _dispatch(ends, dest, h2p.reshape(n, xs, LANES), p_rows)
    b1 = _pair_group(b_moe_in[l])
    b2 = b_moe_out[l].reshape(N_EXPERTS, 1, -1)
    y_buf = _experts(blk_expert.reshape(-1), ends, x_buf.reshape(p_rows * xs, LANES),
                     w_moe_in[l], b1, w_moe_out[l], b2)
    out = _combine(dest, gates, x1, norm_f_g, y_buf.reshape(p_rows, ys, LANES))
    return out.reshape(batch, seq, d)
```

```python
import functools
import math

import jax
import jax.numpy as jnp
from jax import lax
from jax.experimental import pallas as pl
from jax.experimental.pallas import tpu as pltpu

F32 = jnp.float32
BF16 = jnp.bfloat16
I32 = jnp.int32
U32 = jnp.uint32

N_DIFF_HEADS = 4
DIFF_QK_DIM = 64
DIFF_V_DIM = 128
DIFF_WIDTH = N_DIFF_HEADS * DIFF_V_DIM
N_RET_HEADS = 4
RET_QK_DIM = 64
RET_V_DIM = 128
RET_WIDTH = N_RET_HEADS * RET_V_DIM
ROPE_THETA = 10000.0
RMS_EPS = 1e-5
N_EXPERTS = 32
TOP_K = 4
SWIGLU_LIMIT = 7.0
SWIGLU_ALPHA = 1.702
COL_DQ = 0
COL_DK = 512
COL_DV = 1024
COL_RQ = 1536
COL_RK = 1792
COL_RV = 2048
COL_RG = 2560
PROJ_WIDTH = 3072
LOG_DECAY = tuple(math.log(1.0 - 2.0 ** (-5.0 - h)) for h in range(N_RET_HEADS))
LOG2E = 1.4426950408889634

LANES = 128
SUBLANES = 8
MXU_DIM = 256
VMEM_LIMIT = 56 * 1024 * 1024

PROJ_TM = 512
ATT_TQ = 1024
ATT_TK = 1024
RET_T = 512
RET_C = 256
OUT_TM = 1024
PLAN_TC = 512
DISP_TM = 1024
MOE_BM = 512
COMB_TM = 512
COMB_CH = 64

NEG_BIG = -1e30


def _cparams(*sem):
    return pltpu.CompilerParams(dimension_semantics=sem, vmem_limit_bytes=VMEM_LIMIT)


def _rope_tables(pos4, invf):
    nf = DIFF_QK_DIM // 2
    groups = LANES // nf
    lane = lax.broadcasted_iota(I32, (LANES, LANES), 1)
    first_half = (lane & nf) == 0
    stacked = jnp.concatenate(
        [jnp.broadcast_to(pos4[g:g + 1, :].astype(F32), (nf, LANES)) for g in range(groups)], axis=0)
    ang = stacked.T * invf
    c4 = jnp.cos(ang)
    s4 = jnp.sin(ang)
    cos, sin = [], []
    for g in range(groups):
        mine = (lane // nf) == g

        def spread(t):
            z = jnp.where(mine, t, 0.0)
            return z + pltpu.roll(z, nf, 1) + pltpu.roll(z, 2 * nf, 1) + pltpu.roll(z, 3 * nf, 1)

        s = spread(s4)
        cos.append(spread(c4))
        sin.append(jnp.where(first_half, -s, s))
    return jnp.concatenate(cos, axis=0), jnp.concatenate(sin, axis=0)


def _inproj_kernel(x_ref, g_ref, w_ref, pos_ref, invf_ref, o_ref):
    x = x_ref[...]
    ms = jnp.mean(x * x, axis=-1, keepdims=True)
    h = (x * lax.rsqrt(ms + RMS_EPS) * g_ref[...]).astype(BF16)
    groups = PROJ_TM // LANES
    r0 = pl.multiple_of((pl.program_id(0) % (SUBLANES // groups)) * groups, groups)
    cos, sin = _rope_tables(pos_ref[pl.ds(r0, groups), :], invf_ref[...])
    lane = lax.broadcasted_iota(I32, cos.shape, 1)
    nf = DIFF_QK_DIM // 2
    first_half = (lane & nf) == 0

    def rope(t):
        rot = jnp.where(first_half, pltpu.roll(t, LANES - nf, 1), pltpu.roll(t, nf, 1))
        return t * cos + rot * sin

    q_scale = DIFF_QK_DIM ** -0.5 * LOG2E
    k_scale = RET_QK_DIM ** -0.5
    for c in range(PROJ_WIDTH // MXU_DIM):
        p = jnp.dot(h, w_ref[:, c * MXU_DIM:(c + 1) * MXU_DIM], preferred_element_type=F32)
        for half in range(MXU_DIM // LANES):
            col = c * MXU_DIM + half * LANES
            t = p[:, half * LANES:(half + 1) * LANES]
            if col < COL_DK:
                t = rope(t) * q_scale
            elif col < COL_DV:
                t = rope(t)
            elif col < COL_RQ:
                pass
            elif col < COL_RK:
                t = rope(t)
            elif col < COL_RV:
                t = rope(t) * k_scale
            elif col < COL_RG:
                pass
            else:
                t = t * jax.nn.sigmoid(t)
            o_ref[:, col:col + LANES] = t.astype(BF16)


def _in_proj(x2d, g1, w_in_b, positions):
    n, d = x2d.shape
    tm = PROJ_TM
    assert tm == (LANES // (DIFF_QK_DIM // 2)) * LANES and DIFF_QK_DIM == RET_QK_DIM
    pos2d = positions.reshape(n // LANES, LANES)
    inv_freq = 1.0 / (ROPE_THETA ** (jnp.arange(0, DIFF_QK_DIM, 2, dtype=F32) / DIFF_QK_DIM))
    invf = jnp.tile(inv_freq, LANES // (DIFF_QK_DIM // 2)).reshape(1, LANES)
    steps_per_pos_block = SUBLANES * LANES // tm
    return pl.pallas_call(
        _inproj_kernel,
        grid=(n // tm,),
        in_specs=[pl.BlockSpec((tm, d), lambda i: (i, 0)),
                  pl.BlockSpec((1, d), lambda i: (0, 0)),
                  pl.BlockSpec((d, PROJ_WIDTH), lambda i: (0, 0)),
                  pl.BlockSpec((SUBLANES, LANES), lambda i: (i // steps_per_pos_block, 0)),
                  pl.BlockSpec((1, LANES), lambda i: (0, 0))],
        out_specs=pl.BlockSpec((tm, PROJ_WIDTH), lambda i: (i, 0)),
        out_shape=jax.ShapeDtypeStruct((n, PROJ_WIDTH), BF16),
        compiler_params=_cparams("arbitrary"),
        name="in_proj",
    )(x2d, g1.reshape(1, d), w_in_b, pos2d, invf)


def _diff_attn_kernel(lq1_ref, lk1_ref, lq2_ref, lk2_ref, g_ref, q_ref, k_ref, v_ref, o_ref,
                      qq_sc, m_sc, l_sc, acc_sc, *, lam_init):
    tq, tk = ATT_TQ, ATT_TK
    i = pl.program_id(2)
    q = q_ref[...]
    lane = lax.broadcasted_iota(I32, q.shape, 1)
    zero = jnp.zeros_like(q)
    qq_sc[:tq, :] = jnp.where(lane < DIFF_QK_DIM, q, zero)
    qq_sc[tq:, :] = jnp.where(lane >= DIFF_QK_DIM, q, zero)
    m_sc[...] = jnp.full(m_sc.shape, NEG_BIG, F32)
    l_sc[...] = jnp.zeros(l_sc.shape, F32)
    acc_sc[...] = jnp.zeros(acc_sc.shape, F32)

    def tile(rows, start, width, key_off):
        k = k_ref[pl.ds(start, width), :]
        v = v_ref[pl.ds(start, width), :]
        s = lax.dot_general(qq_sc[rows, :], k, (((1,), (1,)), ((), ())), preferred_element_type=F32)
        if key_off is not None:
            row = (lax.broadcasted_iota(I32, s.shape, 0) + rows.start) & (tq - 1)
            col = lax.broadcasted_iota(I32, s.shape, 1) + key_off
            s = jnp.where(col <= row, s, NEG_BIG)
        m_prev = m_sc[rows, :]
        m_next = jnp.maximum(m_prev, jnp.max(s, axis=1, keepdims=True))
        alpha = jnp.exp2(m_prev - m_next)
        p = jnp.exp2(s - jnp.concatenate([m_next] * (width // LANES), axis=1))
        psum = p[:, :LANES]
        for c in range(1, width // LANES):
            psum = psum + p[:, c * LANES:(c + 1) * LANES]
        l_sc[rows, :] = alpha * l_sc[rows, :] + psum
        acc_sc[rows, :] = alpha * acc_sc[rows, :] + jnp.dot(p.astype(BF16), v, preferred_element_type=F32)
        m_sc[rows, :] = m_next

    def body(j, carry):
        for part in range(2):
            tile(slice(part * tq, (part + 1) * tq), pl.multiple_of(j * tk, tk), tk, None)
        return carry

    lax.fori_loop(0, i, body, 0)
    hk = tk // 2
    diag = pl.multiple_of(i * tk, tk)
    for part in range(2):
        tile(slice(part * tq, (part + 1) * tq), diag, hk, 0)
        tile(slice(part * tq + tq // 2, (part + 1) * tq), diag + hk, hk, hk)

    lam = (jnp.exp(jnp.sum(lq1_ref[...] * lk1_ref[...], axis=-1, keepdims=True))
           - jnp.exp(jnp.sum(lq2_ref[...] * lk2_ref[...], axis=-1, keepdims=True))
           + lam_init)
    o = acc_sc[...] / jnp.sum(l_sc[...], axis=1, keepdims=True)
    d = o[:tq, :] - lam * o[tq:, :]
    ms = jnp.mean(d * d, axis=-1, keepdims=True)
    out = d * lax.rsqrt(ms + RMS_EPS) * g_ref[...] * (1.0 - lam_init)
    o_ref[...] = out.astype(BF16)


def _diff_attention(proj, lq1, lk1, lq2, lk2, g, batch, seq, lam_init):
    n = batch * seq
    tq = ATT_TQ
    assert ATT_TQ == ATT_TK and seq % tq == 0 and DIFF_V_DIM == LANES
    nq = seq // tq
    qcol = COL_DQ // LANES
    kcol = COL_DK // LANES
    vcol = COL_DV // LANES
    vec = lambda b, h, i: (0, 0)
    return pl.pallas_call(
        functools.partial(_diff_attn_kernel, lam_init=lam_init),
        grid=(batch, N_DIFF_HEADS, nq),
        in_specs=[pl.BlockSpec((1, DIFF_QK_DIM), vec)] * 4 + [
            pl.BlockSpec((1, DIFF_V_DIM), vec),
            pl.BlockSpec((tq, LANES), lambda b, h, i: (b * nq + i, qcol + h)),
            pl.BlockSpec((seq, LANES), lambda b, h, i: (b, kcol + h)),
            pl.BlockSpec((seq, LANES), lambda b, h, i: (b, vcol + h))],
        out_specs=pl.BlockSpec((tq, LANES), lambda b, h, i: (b * nq + i, h)),
        out_shape=jax.ShapeDtypeStruct((n, DIFF_WIDTH), BF16),
        scratch_shapes=[pltpu.VMEM((2 * tq, LANES), BF16),
                        pltpu.VMEM((2 * tq, LANES), F32),
                        pltpu.VMEM((2 * tq, LANES), F32),
                        pltpu.VMEM((2 * tq, DIFF_V_DIM), F32)],
        compiler_params=_cparams("arbitrary", "arbitrary", "arbitrary"),
        name="diff_attn",
    )(lq1.reshape(1, -1), lk1.reshape(1, -1), lq2.reshape(1, -1), lk2.reshape(1, -1),
      g.reshape(1, -1), proj, proj, proj)


def _retention_kernel(q_ref, k_ref, v_ref, gate_ref, g_ref, o_ref, state_sc, qdec_sc, kdec_sc, intra_sc):
    c_len = RET_C
    lane = lax.broadcasted_iota(I32, (c_len, LANES), 1)
    srow = lax.broadcasted_iota(I32, (LANES, LANES), 0)

    @pl.when((pl.program_id(0) == 0) & (pl.program_id(1) == 0))
    def _():
        ii = lax.broadcasted_iota(I32, (c_len, c_len), 0)
        jj = lax.broadcasted_iota(I32, (c_len, c_len), 1)
        rel = (ii - jj).astype(F32)
        pos = lax.broadcasted_iota(I32, (c_len, LANES), 0).astype(F32)
        for pair in range(N_RET_HEADS // 2):
            ld = (LOG_DECAY[2 * pair], LOG_DECAY[2 * pair + 1])
            ld_lane = jnp.where(lane < RET_QK_DIM, ld[0], ld[1])
            qdec_sc[pair] = jnp.exp(ld_lane * (pos + 1.0))
            kdec_sc[pair] = jnp.exp(ld_lane * (c_len - 1.0 - pos))
            for hh in range(2):
                intra_sc[2 * pair + hh] = jnp.where(rel >= 0, jnp.exp(ld[hh] * jnp.maximum(rel, 0.0)), 0.0)

    @pl.when(pl.program_id(1) == 0)
    def _():
        state_sc[...] = jnp.zeros(state_sc.shape, F32)

    for pair in range(N_RET_HEADS // 2):
        ld = (LOG_DECAY[2 * pair], LOG_DECAY[2 * pair + 1])
        q_decay = qdec_sc[pair]
        k_decay = kdec_sc[pair]
        chunk_decay = jnp.where(srow < RET_QK_DIM, math.exp(ld[0] * c_len), math.exp(ld[1] * c_len))
        intra = [intra_sc[2 * pair], intra_sc[2 * pair + 1]]
        in_head = (lane < RET_QK_DIM, lane >= RET_QK_DIM)
        for c in range(RET_T // c_len):
            rows = slice(c * c_len, (c + 1) * c_len)
            qb = q_ref[rows, pair * LANES:(pair + 1) * LANES]
            kb = k_ref[rows, pair * LANES:(pair + 1) * LANES]
            q = qb.astype(F32)
            state = state_sc[pair]
            state_b = state.astype(BF16)
            kd_t = (kb.astype(F32) * k_decay).T.astype(BF16)
            new_kv = []
            for hh in range(2):
                h = 2 * pair + hh
                qm = jnp.where(in_head[hh], q, 0.0)
                s = lax.dot_general(qm.astype(BF16), kb, (((1,), (1,)), ((), ())),
                                    preferred_element_type=F32) * intra[hh]
                v = v_ref[rows, h * RET_V_DIM:(h + 1) * RET_V_DIM]
                y = (jnp.dot(s.astype(BF16), v, preferred_element_type=F32)
                     + jnp.dot((qm * q_decay).astype(BF16), state_b, preferred_element_type=F32))
                new_kv.append(jnp.dot(kd_t, v, preferred_element_type=F32))
                ms = jnp.mean(y * y, axis=-1, keepdims=True)
                yn = y * lax.rsqrt(ms + RMS_EPS) * g_ref[h:h + 1, :]
                gate = gate_ref[rows, h * RET_V_DIM:(h + 1) * RET_V_DIM].astype(F32)
                o_ref[rows, h * RET_V_DIM:(h + 1) * RET_V_DIM] = (yn * gate).astype(BF16)
            state_sc[pair] = chunk_decay * state + jnp.where(srow < RET_QK_DIM, new_kv[0], new_kv[1])


def _retention(proj, g, batch, seq):
    n = batch * seq
    t = RET_T
    nt = seq // t
    qk_w = N_RET_HEADS * RET_QK_DIM
    return pl.pallas_call(
        _retention_kernel,
        grid=(batch, nt),
        in_specs=[pl.BlockSpec((t, qk_w), lambda b, i: (b * nt + i, COL_RQ // qk_w)),
                  pl.BlockSpec((t, qk_w), lambda b, i: (b * nt + i, COL_RK // qk_w)),
                  pl.BlockSpec((t, RET_WIDTH), lambda b, i: (b * nt + i, COL_RV // RET_WIDTH)),
                  pl.BlockSpec((t, RET_WIDTH), lambda b, i: (b * nt + i, COL_RG // RET_WIDTH)),
                  pl.BlockSpec((N_RET_HEADS, RET_V_DIM), lambda b, i: (0, 0))],
        out_specs=pl.BlockSpec((t, RET_WIDTH), lambda b, i: (b * nt + i, 0)),
        out_shape=jax.ShapeDtypeStruct((n, RET_WIDTH), BF16),
        scratch_shapes=[pltpu.VMEM((N_RET_HEADS // 2, LANES, RET_V_DIM), F32),
                        pltpu.VMEM((N_RET_HEADS // 2, RET_C, LANES), F32),
                        pltpu.VMEM((N_RET_HEADS // 2, RET_C, LANES), F32),
                        pltpu.VMEM((N_RET_HEADS, RET_C, RET_C), F32)],
        compiler_params=_cparams("arbitrary", "arbitrary"),
        name="retention",
    )(proj, proj, proj, proj, g)


def _outproj_kernel(d_ref, r_ref, wo_ref, x_ref, g2_ref, wr_ref, br_ref,
                    x1_ref, h2p_ref, idx_ref, gate_ref, cnt_ref):
    acc = (jnp.dot(d_ref[...], wo_ref[:DIFF_WIDTH, :], preferred_element_type=F32)
           + jnp.dot(r_ref[...], wo_ref[DIFF_WIDTH:, :], preferred_element_type=F32))
    x1 = x_ref[...] + acc
    x1_ref[...] = x1
    ms = jnp.mean(x1 * x1, axis=-1, keepdims=True)
    h2 = x1 * lax.rsqrt(ms + RMS_EPS) * g2_ref[...]
    half = h2.shape[1] // 2
    packed = pltpu.pack_elementwise([h2[:, :half], h2[:, half:]], packed_dtype=BF16)
    slabs = half // LANES
    for j in range(slabs):
        h2p_ref[pl.ds(j, packed.shape[0], stride=slabs), :] = packed[:, j * LANES:(j + 1) * LANES]

    nt = (((1,), (1,)), ((), ()))
    wr = wr_ref[...]
    wr_hi = wr.astype(BF16)
    wr_lo = (wr - wr_hi.astype(F32)).astype(BF16)
    h2_hi = h2.astype(BF16)
    h2_lo = (h2 - h2_hi.astype(F32)).astype(BF16)
    logits = (lax.dot_general(wr_hi, h2_hi, nt, preferred_element_type=F32)
              + lax.dot_general(wr_hi, h2_lo, nt, preferred_element_type=F32)
              + lax.dot_general(wr_lo, h2_hi, nt, preferred_element_type=F32)
              + br_ref[...])
    e_iota = lax.broadcasted_iota(I32, logits.shape, 0)
    vals = []
    chosen = jnp.zeros(logits.shape, F32)
    for r in range(TOP_K):
        m = jnp.max(logits, axis=0, keepdims=True)
        ix = jnp.min(jnp.where(logits == m, e_iota, N_EXPERTS), axis=0, keepdims=True)
        vals.append(m)
        idx_ref[r:r + 1, :] = ix
        hit = e_iota == ix
        chosen = chosen + jnp.where(hit, 1.0, 0.0)
        logits = jnp.where(hit, -jnp.inf, logits)

    @pl.when(pl.program_id(0) == 0)
    def _():
        cnt_ref[...] = jnp.zeros(cnt_ref.shape, F32)

    cnt_ref[...] = cnt_ref[...] + jnp.sum(chosen, axis=1, keepdims=True)
    ex = [jnp.exp(v - vals[0]) for v in vals]
    den = ex[0] + ex[1] + ex[2] + ex[3]
    for r in range(TOP_K):
        gate_ref[r:r + 1, :] = ex[r] / den


def _out_proj(d_out, r_out, w_o_b, x2d, g2, w_router_t, b_router):
    n, d = x2d.shape
    tm = OUT_TM
    const = lambda i: (0, 0)
    return pl.pallas_call(
        _outproj_kernel,
        grid=(n // tm,),
        in_specs=[pl.BlockSpec((tm, DIFF_WIDTH), lambda i: (i, 0)),
                  pl.BlockSpec((tm, RET_WIDTH), lambda i: (i, 0)),
                  pl.BlockSpec((DIFF_WIDTH + RET_WIDTH, d), const),
                  pl.BlockSpec((tm, d), lambda i: (i, 0)),
                  pl.BlockSpec((1, d), const),
                  pl.BlockSpec((N_EXPERTS, d), const),
                  pl.BlockSpec((N_EXPERTS, 1), const)],
        out_specs=[pl.BlockSpec((tm, d), lambda i: (i, 0)),
                   pl.BlockSpec((tm * (d // 2 // LANES), LANES), lambda i: (i, 0)),
                   pl.BlockSpec((TOP_K, tm), lambda i: (0, i)),
                   pl.BlockSpec((TOP_K, tm), lambda i: (0, i)),
                   pl.BlockSpec((N_EXPERTS, 1), const)],
        out_shape=[jax.ShapeDtypeStruct((n, d), F32),
                   jax.ShapeDtypeStruct((n * (d // 2 // LANES), LANES), U32),
                   jax.ShapeDtypeStruct((TOP_K, n), I32),
                   jax.ShapeDtypeStruct((TOP_K, n), F32),
                   jax.ShapeDtypeStruct((N_EXPERTS, 1), F32)],
        compiler_params=_cparams("arbitrary"),
        name="out_proj",
    )(d_out, r_out, w_o_b, x2d, g2.reshape(1, d), w_router_t, b_router.reshape(N_EXPERTS, 1))


def _plan_kernel(idx_ref, cnt_ref, dest_ref, blk_ref, ends_ref, base_sc, tri_sc, *, nb_pad):
    c = pl.program_id(0)
    tc = PLAN_TC
    e_iota = lax.broadcasted_iota(I32, (N_EXPERTS, tc), 0)

    @pl.when(c == 0)
    def _():
        s = lax.broadcasted_iota(I32, (tc, tc), 0)
        t = lax.broadcasted_iota(I32, (tc, tc), 1)
        tri_sc[...] = jnp.where(s < t, 1.0, 0.0).astype(BF16)
        cnt = cnt_ref[...]
        nblk = jnp.floor((cnt + (MOE_BM - 1.0)) * (1.0 / MOE_BM))
        ei = lax.broadcasted_iota(I32, (N_EXPERTS, LANES), 0)
        li = lax.broadcasted_iota(I32, (N_EXPERTS, LANES), 1)
        nblk_row = jnp.sum(jnp.where(ei == li, nblk, 0.0), axis=0, keepdims=True)
        start = jnp.sum(jnp.where(li < ei, nblk_row, 0.0), axis=1, keepdims=True)
        base_sc[...] = start * MOE_BM
        end = start + nblk
        bi = lax.broadcasted_iota(I32, (N_EXPERTS, nb_pad), 1).astype(F32)
        be = jnp.sum(jnp.where(end <= bi, 1.0, 0.0), axis=0, keepdims=True)
        blk_ref[...] = jnp.minimum(be, N_EXPERTS - 1.0).astype(I32)
        ends_ref[...] = jnp.sum(jnp.where(ei == li, end, 0.0), axis=0, keepdims=True).astype(I32)

    base = base_sc[...]
    for k in range(TOP_K):
        oh = idx_ref[k:k + 1, :] == e_iota
        ohb = jnp.where(oh, 1.0, 0.0).astype(BF16)
        before = jnp.dot(ohb, tri_sc[...], preferred_element_type=F32)
        rank = jnp.sum(jnp.where(oh, before + base, 0.0), axis=0, keepdims=True).astype(I32)
        for g in range(tc // LANES):
            dest_ref[g * TOP_K + k:g * TOP_K + k + 1, :] = rank[:, g * LANES:(g + 1) * LANES]
        base = base + jnp.sum(jnp.where(oh, 1.0, 0.0), axis=1, keepdims=True)
    base_sc[...] = base


def _plan(top_idx, counts, nb_pad):
    n = top_idx.shape[1]
    tc = PLAN_TC
    return pl.pallas_call(
        functools.partial(_plan_kernel, nb_pad=nb_pad),
        grid=(n // tc,),
        in_specs=[pl.BlockSpec((TOP_K, tc), lambda c: (0, c)),
                  pl.BlockSpec((N_EXPERTS, 1), lambda c: (0, 0))],
        out_specs=[pl.BlockSpec((tc // LANES * TOP_K, LANES), lambda c: (c, 0)),
                   pl.BlockSpec((1, nb_pad), lambda c: (0, 0)),
                   pl.BlockSpec((1, LANES), lambda c: (0, 0))],
        out_shape=[jax.ShapeDtypeStruct((n // LANES * TOP_K, LANES), I32),
                   jax.ShapeDtypeStruct((1, nb_pad), I32),
                   jax.ShapeDtypeStruct((1, LANES), I32)],
        scratch_shapes=[pltpu.VMEM((N_EXPERTS, 1), F32),
                        pltpu.VMEM((tc, tc), BF16)],
        compiler_params=_cparams("arbitrary"),
        name="plan",
    )(top_idx, counts)


def _dispatch_kernel(ends_ref, dest_ref, h_ref, xbuf_ref, zbuf, sem, zsem, *, nb):
    tm = DISP_TM
    bm = MOE_BM

    @pl.when(pl.program_id(0) == 0)
    def _():
        zbuf[...] = jnp.zeros(zbuf.shape, zbuf.dtype)

        def zero_block(blk):
            return pltpu.make_async_copy(zbuf, xbuf_ref.at[pl.ds(pl.multiple_of(blk * bm, bm), bm)], zsem)

        def per_block(fn):
            for e in range(N_EXPERTS):
                end = ends_ref[e]
                first = ends_ref[e - 1] if e else 0

                @pl.when(end > first)
                def _():
                    fn(zero_block(end - 1))

            def tail(blk, carry):
                fn(zero_block(blk))
                return carry

            lax.fori_loop(ends_ref[N_EXPERTS - 1], nb, tail, 0)

        per_block(lambda cp: cp.start())
        per_block(lambda cp: cp.wait())

    def issue(g, carry):
        for u in range(LANES):
            for k in range(TOP_K):
                src = h_ref.at[g * LANES + u]
                dst = xbuf_ref.at[dest_ref[g * TOP_K + k, u]]
                pltpu.make_async_copy(src, dst, sem).start(priority=k % 2)
        return carry

    lax.fori_loop(0, tm // LANES, issue, 0)
    for k in range(TOP_K):
        pltpu.make_async_copy(xbuf_ref.at[pl.ds(0, tm)], xbuf_ref.at[pl.ds(0, tm)], sem).wait()


def _dispatch(ends, dest, h2p, p_rows):
    n, s, w = h2p.shape
    tm = DISP_TM
    grid_spec = pltpu.PrefetchScalarGridSpec(
        num_scalar_prefetch=1,
        grid=(n // tm,),
        in_specs=[pl.BlockSpec((tm // LANES * TOP_K, LANES), lambda i, ends: (i, 0), memory_space=pltpu.SMEM),
                  pl.BlockSpec((tm, s, w), lambda i, ends: (i, 0, 0))],
        out_specs=pl.BlockSpec(memory_space=pl.ANY),
        scratch_shapes=[pltpu.VMEM((MOE_BM, s, w), h2p.dtype),
                        pltpu.SemaphoreType.DMA(()),
                        pltpu.SemaphoreType.DMA(())],
    )
    return pl.pallas_call(
        functools.partial(_dispatch_kernel, nb=p_rows // MOE_BM),
        grid_spec=grid_spec,
        out_shape=jax.ShapeDtypeStruct((p_rows, s, w), h2p.dtype),
        compiler_params=pltpu.CompilerParams(dimension_semantics=("arbitrary",),
                                             vmem_limit_bytes=VMEM_LIMIT,
                                             has_side_effects=True),
        name="dispatch",
    )(ends, dest, h2p)


def _pair_perm():
    a = lax.broadcasted_iota(I32, (MXU_DIM, MXU_DIM), 0)
    b = lax.broadcasted_iota(I32, (MXU_DIM, MXU_DIM), 1)
    src = jnp.where(b < LANES, 2 * b, 2 * (b - LANES) + 1)
    return jnp.where(a == src, 1.0, 0.0).astype(BF16)


def _pair_group(b):
    e, f2 = b.shape
    return b.reshape(e, f2 // MXU_DIM, LANES, 2).transpose(0, 1, 3, 2).reshape(e, 1, f2)


def _expert_kernel(blk_ref, ends_ref, x_ref, w1_hbm, b1_ref, w2_hbm, b2_ref, y_ref,
                   w1f, w2f, w1_ref, w2_ref, wsem, slot_ref):
    b = pl.program_id(0)
    n_used = ends_ref[N_EXPERTS - 1]

    @pl.when(b >= n_used)
    def _():
        zero = jnp.zeros(y_ref.shape, F32)
        y_ref[...] = pltpu.pack_elementwise([zero, zero], packed_dtype=BF16)

    def fetch(expert, slot):
        return (pltpu.make_async_copy(w1_hbm.at[expert], w1f.at[slot], wsem.at[slot]),
                pltpu.make_async_copy(w2_hbm.at[expert], w2f.at[slot], wsem.at[slot]))

    @pl.when(b == 0)
    def _():
        slot_ref[0] = 0
        for cp in fetch(blk_ref[0], 0):
            cp.start()

    new_expert = (b == 0) | (blk_ref[b] != blk_ref[jnp.maximum(b - 1, 0)])

    @pl.when((b < n_used) & new_expert)
    def _():
        slot = slot_ref[0]
        expert = blk_ref[b]
        for cp in fetch(expert, slot):
            cp.wait()
        next_first = ends_ref[expert]

        @pl.when(next_first < n_used)
        def _():
            for cp in fetch(blk_ref[next_first], 1 - slot):
                cp.start()

        perm = _pair_perm()
        for c in range(w1f.shape[2] // MXU_DIM):
            cols = slice(c * MXU_DIM, (c + 1) * MXU_DIM)
            blk = w1f[slot, :, cols].astype(BF16)
            w1_ref[0, :, cols] = jnp.dot(blk, perm, preferred_element_type=F32).astype(BF16)
        w2_ref[0] = w2f[slot].astype(BF16)
        slot_ref[0] = 1 - slot

    @pl.when(b < n_used)
    def _():
        ns = w2_ref.shape[2] // 2 // LANES
        bm = x_ref.shape[0] // ns
        slabs = [x_ref[pl.ds(j, bm, stride=ns), :] for j in range(ns)]
        lo = [pltpu.unpack_elementwise(w, index=0, packed_dtype=BF16, unpacked_dtype=F32).astype(BF16)
              for w in slabs]
        hi = [pltpu.unpack_elementwise(w, index=1, packed_dtype=BF16, unpacked_dtype=F32).astype(BF16)
              for w in slabs]
        x = jnp.concatenate(lo + hi, axis=1)
        acc = jnp.zeros((x.shape[0], w2_ref.shape[2]), F32)
        grp = 2 * MXU_DIM
        for c in range(w1_ref.shape[2] // grp):
            cols = slice(c * grp, (c + 1) * grp)
            h = jnp.dot(x, w1_ref[0, :, cols], preferred_element_type=F32) + b1_ref[0, :, cols]
            glu = jnp.concatenate([h[:, 0:LANES], h[:, 2 * LANES:3 * LANES]], axis=1)
            lin = jnp.concatenate([h[:, LANES:2 * LANES], h[:, 3 * LANES:4 * LANES]], axis=1)
            glu = jnp.minimum(glu, SWIGLU_LIMIT)
            lin = jnp.clip(lin, -SWIGLU_LIMIT, SWIGLU_LIMIT)
            act = glu * jax.nn.sigmoid(SWIGLU_ALPHA * glu) * (lin + 1.0)
            acc = acc + jnp.dot(act.astype(BF16), w2_ref[0, c * MXU_DIM:(c + 1) * MXU_DIM, :],
                                preferred_element_type=F32)
        y = acc + b2_ref[0]
        half = y.shape[1] // 2
        packed = pltpu.pack_elementwise([y[:, :half], y[:, half:]], packed_dtype=BF16)
        for j in range(ns):
            y_ref[pl.ds(j, bm, stride=ns), :] = packed[:, j * LANES:(j + 1) * LANES]


def _experts(blk_expert, ends, x_buf, w1, b1, w2, b2):
    e, d, f2 = w1.shape
    xs = d // 2 // LANES
    ys = xs
    p_rows = x_buf.shape[0] // xs
    f = f2 // 2
    bm = MOE_BM
    nb = p_rows // bm

    def row_blk(b, ends):
        return jnp.minimum(b, ends[N_EXPERTS - 1] - 1)

    grid_spec = pltpu.PrefetchScalarGridSpec(
        num_scalar_prefetch=2,
        grid=(nb,),
        in_specs=[pl.BlockSpec((bm * xs, LANES), lambda b, blk, ends: (row_blk(b, ends), 0)),
                  pl.BlockSpec(memory_space=pl.ANY),
                  pl.BlockSpec((1, 1, f2), lambda b, blk, ends: (blk[row_blk(b, ends)], 0, 0)),
                  pl.BlockSpec(memory_space=pl.ANY),
                  pl.BlockSpec((1, 1, d), lambda b, blk, ends: (blk[row_blk(b, ends)], 0, 0))],
        out_specs=pl.BlockSpec((bm * ys, LANES), lambda b, blk, ends: (b, 0)),
        scratch_shapes=[pltpu.VMEM((2, d, f2), F32),
                        pltpu.VMEM((2, f, d), F32),
                        pltpu.VMEM((1, d, f2), BF16),
                        pltpu.VMEM((1, f, d), BF16),
                        pltpu.SemaphoreType.DMA((2,)),
                        pltpu.SMEM((1,), I32)],
    )
    return pl.pallas_call(
        _expert_kernel,
        grid_spec=grid_spec,
        out_shape=jax.ShapeDtypeStruct((p_rows * ys, LANES), x_buf.dtype),
        compiler_params=_cparams("arbitrary"),
        name="experts",
    )(blk_expert, ends, x_buf, w1, b1, w2, b2)


def _combine_kernel(dest_ref, next_dest_ref, gate_ref, x1_ref, gf_ref, ybuf_ref, o_ref, gbuf, sem):
    tm = COMB_TM
    i = pl.program_id(0)
    cur = i % 2

    ys = ybuf_ref.shape[1]

    def gather(idx_ref, buf):
        def issue(g, carry):
            for u in range(LANES):
                for k in range(TOP_K):
                    src = ybuf_ref.at[idx_ref[g * TOP_K + k, u]]
                    row = pl.multiple_of(g * (LANES * ys) + u * ys, ys)
                    dst = gbuf.at[buf, k, pl.ds(row, ys), :]
                    pltpu.make_async_copy(src, dst, sem.at[buf]).start(priority=k % 2)
            return carry

        lax.fori_loop(0, tm // LANES, issue, 0)

    @pl.when(i == 0)
    def _():
        gather(dest_ref, 0)

    @pl.when(i + 1 < pl.num_programs(0))
    def _():
        gather(next_dest_ref, 1 - cur)

    for k in range(TOP_K):
        pltpu.make_async_copy(ybuf_ref.at[pl.ds(0, tm)], ybuf_ref.at[pl.ds(0, tm)], sem.at[cur]).wait()

    gates = gate_ref[...]
    pad = jnp.zeros((LANES - TOP_K, LANES), F32)
    cols = []
    for c in range(tm // LANES):
        blk = jnp.concatenate([gates[:, c * LANES:(c + 1) * LANES], pad], axis=0)
        cols.append(blk.T)
    gcol = jnp.concatenate(cols, axis=0)
    for c in range(tm // COMB_CH):
        rows = slice(c * COMB_CH, (c + 1) * COMB_CH)
        gk = [gcol[rows, k:k + 1] for k in range(TOP_K)]
        sq = jnp.zeros((COMB_CH, LANES), F32)
        for j in range(ys):
            words = [gbuf[cur, k, pl.ds(c * COMB_CH * ys + j, COMB_CH, stride=ys), :] for k in range(TOP_K)]
            for part in range(2):
                col = (part * ys + j) * LANES
                a = x1_ref[rows, col:col + LANES]
                for k in range(TOP_K):
                    yk = pltpu.unpack_elementwise(words[k], index=part, packed_dtype=BF16, unpacked_dtype=F32)
                    a = a + yk * gk[k]
                sq = sq + a * a
                o_ref[rows, col:col + LANES] = a
        ms = jnp.sum(sq, axis=-1, keepdims=True) * (1.0 / o_ref.shape[1])
        o_ref[rows, :] = o_ref[rows, :] * lax.rsqrt(ms + RMS_EPS) * gf_ref[...]


def _combine(dest, gates, x1, gf, y_buf):
    n, d = x1.shape
    tm = COMB_TM
    last = n // tm - 1
    idx_rows = tm // LANES * TOP_K
    return pl.pallas_call(
        _combine_kernel,
        grid=(n // tm,),
        in_specs=[pl.BlockSpec((idx_rows, LANES), lambda i: (i, 0), memory_space=pltpu.SMEM),
                  pl.BlockSpec((idx_rows, LANES), lambda i: (jnp.minimum(i + 1, last), 0), memory_space=pltpu.SMEM),
                  pl.BlockSpec((TOP_K, tm), lambda i: (0, i)),
                  pl.BlockSpec((tm, d), lambda i: (i, 0)),
                  pl.BlockSpec((1, d), lambda i: (0, 0)),
                  pl.BlockSpec(memory_space=pl.ANY)],
        out_specs=pl.BlockSpec((tm, d), lambda i: (i, 0)),
        out_shape=jax.ShapeDtypeStruct((n, d), F32),
        scratch_shapes=[pltpu.VMEM((2, TOP_K, tm * y_buf.shape[1], LANES), y_buf.dtype),
                        pltpu.SemaphoreType.DMA((2,))],
        compiler_params=_cparams("arbitrary"),
        name="combine",
    )(dest, dest, gates, x1, gf.reshape(1, d), y_buf)


def kernel(x, positions, norm1_g, w_in, lambda_q1, lambda_k1, lambda_q2, lambda_k2, diff_norm_g, ret_norm_g, w_o, norm2_g, w_router, b_router, w_moe_in, b_moe_in, w_moe_out, b_moe_out, norm_f_g):
    batch, seq, d = x.shape
    n = batch * seq
    assert norm1_g.shape[0] == 1, "single-layer block"
    l = 0
    p_rows = n * TOP_K + N_EXPERTS * MOE_BM
    nb = p_rows // MOE_BM
    nb_pad = -(-nb // LANES) * LANES

    x2d = x.reshape(n, d)
    lam_init = 0.8 - 0.6 * math.exp(-0.3 * l)
    proj = _in_proj(x2d, norm1_g[l], w_in[l].astype(BF16), positions)
    d_out = _diff_attention(proj, lambda_q1[l], lambda_k1[l], lambda_q2[l], lambda_k2[l],
                            diff_norm_g[l], batch, seq, lam_init)
    r_out = _retention(proj, ret_norm_g[l], batch, seq)
    x1, h2p, top_idx, gates, counts = _out_proj(d_out, r_out, w_o[l].astype(BF16), x2d, norm2_g[l],
                                                w_router[l].T, b_router[l])
    dest, blk_expert, ends = _plan(top_idx, counts, nb_pad)
    ends = ends.reshape(-1)
    xs = ys = d // 2 // LANES
    x_buf = _dispatch(ends, dest, h2p.reshape(n, xs, LANES), p_rows)
    b1 = _pair_group(b_moe_in[l])
    b2 = b_moe_out[l].reshape(N_EXPERTS, 1, -1)
    y_buf = _experts(blk_expert.reshape(-1), ends, x_buf.reshape(p_rows * xs, LANES),
                     w_moe_in[l], b1, w_moe_out[l], b2)
    out = _combine(dest, gates, x1, norm_f_g, y_buf.reshape(p_rows, ys, LANES))
    return out.reshape(batch, seq, d)
```

```python
import functools
import math

import jax
import jax.numpy as jnp
from jax import lax
from jax.experimental import pallas as pl
from jax.experimental.pallas import tpu as pltpu

F32 = jnp.float32
BF16 = jnp.bfloat16
I32 = jnp.int32
U32 = jnp.uint32

N_DIFF_HEADS = 4
DIFF_QK_DIM = 64
DIFF_V_DIM = 128
DIFF_WIDTH = N_DIFF_HEADS * DIFF_V_DIM
N_RET_HEADS = 4
RET_QK_DIM = 64
RET_V_DIM = 128
RET_WIDTH = N_RET_HEADS * RET_V_DIM
ROPE_THETA = 10000.0
RMS_EPS = 1e-5
N_EXPERTS = 32
TOP_K = 4
SWIGLU_LIMIT = 7.0
SWIGLU_ALPHA = 1.702
COL_DQ = 0
COL_DK = 512
COL_DV = 1024
COL_RQ = 1536
COL_RK = 1792
COL_RV = 2048
COL_RG = 2560
PROJ_WIDTH = 3072
LOG_DECAY = tuple(math.log(1.0 - 2.0 ** (-5.0 - h)) for h in range(N_RET_HEADS))
LOG2E = 1.4426950408889634

LANES = 128
SUBLANES = 8
MXU_DIM = 256
VMEM_LIMIT = 56 * 1024 * 1024

PROJ_TM = 512
ATT_TQ = 1024
ATT_TK = 1024
RET_T = 512
RET_C = 256
OUT_TM = 1024
PLAN_TC = 512
DISP_TM = 2048
MOE_BM = 512
COMB_TM = 1024
COMB_CH = 64

NEG_BIG = -1e30


def _cparams(*sem):
    return pltpu.CompilerParams(dimension_semantics=sem, vmem_limit_bytes=VMEM_LIMIT)


def _rope_tables(pos4, invf):
    nf = DIFF_QK_DIM // 2
    groups = LANES // nf
    lane = lax.broadcasted_iota(I32, (LANES, LANES), 1)
    first_half = (lane & nf) == 0
    stacked = jnp.concatenate(
        [jnp.broadcast_to(pos4[g:g + 1, :].astype(F32), (nf, LANES)) for g in range(groups)], axis=0)
    ang = stacked.T * invf
    c4 = jnp.cos(ang)
    s4 = jnp.sin(ang)
    cos, sin = [], []
    for g in range(groups):
        mine = (lane // nf) == g

        def spread(t):
            z = jnp.where(mine, t, 0.0)
            return z + pltpu.roll(z, nf, 1) + pltpu.roll(z, 2 * nf, 1) + pltpu.roll(z, 3 * nf, 1)

        s = spread(s4)
        cos.append(spread(c4))
        sin.append(jnp.where(first_half, -s, s))
    return jnp.concatenate(cos, axis=0), jnp.concatenate(sin, axis=0)


def _inproj_kernel(x_ref, g_ref, w_ref, pos_ref, invf_ref, o_ref):
    x = x_ref[...]
    ms = jnp.mean(x * x, axis=-1, keepdims=True)
    h = (x * lax.rsqrt(ms + RMS_EPS) * g_ref[...]).astype(BF16)
    groups = PROJ_TM // LANES
    r0 = pl.multiple_of((pl.program_id(0) % (SUBLANES // groups)) * groups, groups)
    cos, sin = _rope_tables(pos_ref[pl.ds(r0, groups), :], invf_ref[...])
    lane = lax.broadcasted_iota(I32, cos.shape, 1)
    nf = DIFF_QK_DIM // 2
    first_half = (lane & nf) == 0

    def rope(t):
        rot = jnp.where(first_half, pltpu.roll(t, LANES - nf, 1), pltpu.roll(t, nf, 1))
        return t * cos + rot * sin

    q_scale = DIFF_QK_DIM ** -0.5 * LOG2E
    k_scale = RET_QK_DIM ** -0.5
    for c in range(PROJ_WIDTH // MXU_DIM):
        p = jnp.dot(h, w_ref[:, c * MXU_DIM:(c + 1) * MXU_DIM], preferred_element_type=F32)
        for half in range(MXU_DIM // LANES):
            col = c * MXU_DIM + half * LANES
            t = p[:, half * LANES:(half + 1) * LANES]
            if col < COL_DK:
                t = rope(t) * q_scale
            elif col < COL_DV:
                t = rope(t)
            elif col < COL_RQ:
                pass
            elif col < COL_RK:
                t = rope(t)
            elif col < COL_RV:
                t = rope(t) * k_scale
            elif col < COL_RG:
                pass
            else:
                t = t * jax.nn.sigmoid(t)
            o_ref[:, col:col + LANES] = t.astype(BF16)


def _in_proj(x2d, g1, w_in_b, positions):
    n, d = x2d.shape
    tm = PROJ_TM
    assert tm == (LANES // (DIFF_QK_DIM // 2)) * LANES and DIFF_QK_DIM == RET_QK_DIM
    pos2d = positions.reshape(n // LANES, LANES)
    inv_freq = 1.0 / (ROPE_THETA ** (jnp.arange(0, DIFF_QK_DIM, 2, dtype=F32) / DIFF_QK_DIM))
    invf = jnp.tile(inv_freq, LANES // (DIFF_QK_DIM // 2)).reshape(1, LANES)
    steps_per_pos_block = SUBLANES * LANES // tm
    return pl.pallas_call(
        _inproj_kernel,
        grid=(n // tm,),
        in_specs=[pl.BlockSpec((tm, d), lambda i: (i, 0)),
                  pl.BlockSpec((1, d), lambda i: (0, 0)),
                  pl.BlockSpec((d, PROJ_WIDTH), lambda i: (0, 0)),
                  pl.BlockSpec((SUBLANES, LANES), lambda i: (i // steps_per_pos_block, 0)),
                  pl.BlockSpec((1, LANES), lambda i: (0, 0))],
        out_specs=pl.BlockSpec((tm, PROJ_WIDTH), lambda i: (i, 0)),
        out_shape=jax.ShapeDtypeStruct((n, PROJ_WIDTH), BF16),
        compiler_params=_cparams("arbitrary"),
        name="in_proj",
    )(x2d, g1.reshape(1, d), w_in_b, pos2d, invf)


def _diff_attn_kernel(lq1_ref, lk1_ref, lq2_ref, lk2_ref, g_ref, q_ref, k_ref, v_ref, o_ref,
                      qq_sc, m_sc, l_sc, acc_sc, *, lam_init):
    tq, tk = ATT_TQ, ATT_TK
    i = pl.program_id(2)
    q = q_ref[...]
    lane = lax.broadcasted_iota(I32, q.shape, 1)
    zero = jnp.zeros_like(q)
    qq_sc[:tq, :] = jnp.where(lane < DIFF_QK_DIM, q, zero)
    qq_sc[tq:, :] = jnp.where(lane >= DIFF_QK_DIM, q, zero)
    m_sc[...] = jnp.full(m_sc.shape, NEG_BIG, F32)
    l_sc[...] = jnp.zeros(l_sc.shape, F32)
    acc_sc[...] = jnp.zeros(acc_sc.shape, F32)

    def tile(rows, start, width, key_off):
        k = k_ref[pl.ds(start, width), :]
        v = v_ref[pl.ds(start, width), :]
        s = lax.dot_general(qq_sc[rows, :], k, (((1,), (1,)), ((), ())), preferred_element_type=F32)
        if key_off is not None:
            row = (lax.broadcasted_iota(I32, s.shape, 0) + rows.start) & (tq - 1)
            col = lax.broadcasted_iota(I32, s.shape, 1) + key_off
            s = jnp.where(col <= row, s, NEG_BIG)
        m_prev = m_sc[rows, :]
        m_next = jnp.maximum(m_prev, jnp.max(s, axis=1, keepdims=True))
        alpha = jnp.exp2(m_prev - m_next)
        p = jnp.exp2(s - jnp.concatenate([m_next] * (width // LANES), axis=1))
        psum = p[:, :LANES]
        for c in range(1, width // LANES):
            psum = psum + p[:, c * LANES:(c + 1) * LANES]
        l_sc[rows, :] = alpha * l_sc[rows, :] + psum
        acc_sc[rows, :] = alpha * acc_sc[rows, :] + jnp.dot(p.astype(BF16), v, preferred_element_type=F32)
        m_sc[rows, :] = m_next

    def body(j, carry):
        for part in range(2):
            tile(slice(part * tq, (part + 1) * tq), pl.multiple_of(j * tk, tk), tk, None)
        return carry

    lax.fori_loop(0, i, body, 0)
    hk = tk // 2
    diag = pl.multiple_of(i * tk, tk)
    for part in range(2):
        tile(slice(part * tq, (part + 1) * tq), diag, hk, 0)
        tile(slice(part * tq + tq // 2, (part + 1) * tq), diag + hk, hk, hk)

    lam = (jnp.exp(jnp.sum(lq1_ref[...] * lk1_ref[...], axis=-1, keepdims=True))
           - jnp.exp(jnp.sum(lq2_ref[...] * lk2_ref[...], axis=-1, keepdims=True))
           + lam_init)
    o = acc_sc[...] / jnp.sum(l_sc[...], axis=1, keepdims=True)
    d = o[:tq, :] - lam * o[tq:, :]
    ms = jnp.mean(d * d, axis=-1, keepdims=True)
    out = d * lax.rsqrt(ms + RMS_EPS) * g_ref[...] * (1.0 - lam_init)
    o_ref[...] = out.astype(BF16)


def _diff_attention(proj, lq1, lk1, lq2, lk2, g, batch, seq, lam_init):
    n = batch * seq
    tq = ATT_TQ
    assert ATT_TQ == ATT_TK and seq % tq == 0 and DIFF_V_DIM == LANES
    nq = seq // tq
    qcol = COL_DQ // LANES
    kcol = COL_DK // LANES
    vcol = COL_DV // LANES
    vec = lambda b, h, i: (0, 0)
    return pl.pallas_call(
        functools.partial(_diff_attn_kernel, lam_init=lam_init),
        grid=(batch, N_DIFF_HEADS, nq),
        in_specs=[pl.BlockSpec((1, DIFF_QK_DIM), vec)] * 4 + [
            pl.BlockSpec((1, DIFF_V_DIM), vec),
            pl.BlockSpec((tq, LANES), lambda b, h, i: (b * nq + i, qcol + h)),
            pl.BlockSpec((seq, LANES), lambda b, h, i: (b, kcol + h)),
            pl.BlockSpec((seq, LANES), lambda b, h, i: (b, vcol + h))],
        out_specs=pl.BlockSpec((tq, LANES), lambda b, h, i: (b * nq + i, h)),
        out_shape=jax.ShapeDtypeStruct((n, DIFF_WIDTH), BF16),
        scratch_shapes=[pltpu.VMEM((2 * tq, LANES), BF16),
                        pltpu.VMEM((2 * tq, LANES), F32),
                        pltpu.VMEM((2 * tq, LANES), F32),
                        pltpu.VMEM((2 * tq, DIFF_V_DIM), F32)],
        compiler_params=_cparams("arbitrary", "arbitrary", "arbitrary"),
        name="diff_attn",
    )(lq1.reshape(1, -1), lk1.reshape(1, -1), lq2.reshape(1, -1), lk2.reshape(1, -1),
      g.reshape(1, -1), proj, proj, proj)


def _retention_kernel(q_ref, k_ref, v_ref, gate_ref, g_ref, o_ref, state_sc, qdec_sc, kdec_sc, intra_sc):
    c_len = RET_C
    lane = lax.broadcasted_iota(I32, (c_len, LANES), 1)
    srow = lax.broadcasted_iota(I32, (LANES, LANES), 0)

    @pl.when((pl.program_id(0) == 0) & (pl.program_id(1) == 0))
    def _():
        ii = lax.broadcasted_iota(I32, (c_len, c_len), 0)
        jj = lax.broadcasted_iota(I32, (c_len, c_len), 1)
        rel = (ii - jj).astype(F32)
        pos = lax.broadcasted_iota(I32, (c_len, LANES), 0).astype(F32)
        for pair in range(N_RET_HEADS // 2):
            ld = (LOG_DECAY[2 * pair], LOG_DECAY[2 * pair + 1])
            ld_lane = jnp.where(lane < RET_QK_DIM, ld[0], ld[1])
            qdec_sc[pair] = jnp.exp(ld_lane * (pos + 1.0))
            kdec_sc[pair] = jnp.exp(ld_lane * (c_len - 1.0 - pos))
            for hh in range(2):
                intra_sc[2 * pair + hh] = jnp.where(rel >= 0, jnp.exp(ld[hh] * jnp.maximum(rel, 0.0)), 0.0)

    @pl.when(pl.program_id(1) == 0)
    def _():
        state_sc[...] = jnp.zeros(state_sc.shape, F32)

    for pair in range(N_RET_HEADS // 2):
        ld = (LOG_DECAY[2 * pair], LOG_DECAY[2 * pair + 1])
        q_decay = qdec_sc[pair]
        k_decay = kdec_sc[pair]
        chunk_decay = jnp.where(srow < RET_QK_DIM, math.exp(ld[0] * c_len), math.exp(ld[1] * c_len))
        intra = [intra_sc[2 * pair], intra_sc[2 * pair + 1]]
        in_head = (lane < RET_QK_DIM, lane >= RET_QK_DIM)
        for c in range(RET_T // c_len):
            rows = slice(c * c_len, (c + 1) * c_len)
            qb = q_ref[rows, pair * LANES:(pair + 1) * LANES]
            kb = k_ref[rows, pair * LANES:(pair + 1) * LANES]
            q = qb.astype(F32)
            state = state_sc[pair]
            state_b = state.astype(BF16)
            kd_t = (kb.astype(F32) * k_decay).T.astype(BF16)
            new_kv = []
            for hh in range(2):
                h = 2 * pair + hh
                qm = jnp.where(in_head[hh], q, 0.0)
                s = lax.dot_general(qm.astype(BF16), kb, (((1,), (1,)), ((), ())),
                                    preferred_element_type=F32) * intra[hh]
                v = v_ref[rows, h * RET_V_DIM:(h + 1) * RET_V_DIM]
                y = (jnp.dot(s.astype(BF16), v, preferred_element_type=F32)
                     + jnp.dot((qm * q_decay).astype(BF16), state_b, preferred_element_type=F32))
                new_kv.append(jnp.dot(kd_t, v, preferred_element_type=F32))
                ms = jnp.mean(y * y, axis=-1, keepdims=True)
                yn = y * lax.rsqrt(ms + RMS_EPS) * g_ref[h:h + 1, :]
                gate = gate_ref[rows, h * RET_V_DIM:(h + 1) * RET_V_DIM].astype(F32)
                o_ref[rows, h * RET_V_DIM:(h + 1) * RET_V_DIM] = (yn * gate).astype(BF16)
            state_sc[pair] = chunk_decay * state + jnp.where(srow < RET_QK_DIM, new_kv[0], new_kv[1])


def _retention(proj, g, batch, seq):
    n = batch * seq
    t = RET_T
    nt = seq // t
    qk_w = N_RET_HEADS * RET_QK_DIM
    return pl.pallas_call(
        _retention_kernel,
        grid=(batch, nt),
        in_specs=[pl.BlockSpec((t, qk_w), lambda b, i: (b * nt + i, COL_RQ // qk_w)),
                  pl.BlockSpec((t, qk_w), lambda b, i: (b * nt + i, COL_RK // qk_w)),
                  pl.BlockSpec((t, RET_WIDTH), lambda b, i: (b * nt + i, COL_RV // RET_WIDTH)),
                  pl.BlockSpec((t, RET_WIDTH), lambda b, i: (b * nt + i, COL_RG // RET_WIDTH)),
                  pl.BlockSpec((N_RET_HEADS, RET_V_DIM), lambda b, i: (0, 0))],
        out_specs=pl.BlockSpec((t, RET_WIDTH), lambda b, i: (b * nt + i, 0)),
        out_shape=jax.ShapeDtypeStruct((n, RET_WIDTH), BF16),
        scratch_shapes=[pltpu.VMEM((N_RET_HEADS // 2, LANES, RET_V_DIM), F32),
                        pltpu.VMEM((N_RET_HEADS // 2, RET_C, LANES), F32),
                        pltpu.VMEM((N_RET_HEADS // 2, RET_C, LANES), F32),
                        pltpu.VMEM((N_RET_HEADS, RET_C, RET_C), F32)],
        compiler_params=_cparams("arbitrary", "arbitrary"),
        name="retention",
    )(proj, proj, proj, proj, g)


def _outproj_kernel(d_ref, r_ref, wo_ref, x_ref, g2_ref, wr_ref, br_ref,
                    x1_ref, h2p_ref, idx_ref, gate_ref, cnt_ref):
    acc = (jnp.dot(d_ref[...], wo_ref[:DIFF_WIDTH, :], preferred_element_type=F32)
           + jnp.dot(r_ref[...], wo_ref[DIFF_WIDTH:, :], preferred_element_type=F32))
    x1 = x_ref[...] + acc
    x1_ref[...] = x1
    ms = jnp.mean(x1 * x1, axis=-1, keepdims=True)
    h2 = x1 * lax.rsqrt(ms + RMS_EPS) * g2_ref[...]
    half = h2.shape[1] // 2
    packed = pltpu.pack_elementwise([h2[:, :half], h2[:, half:]], packed_dtype=BF16)
    slabs = half // LANES
    for j in range(slabs):
        h2p_ref[pl.ds(j, packed.shape[0], stride=slabs), :] = packed[:, j * LANES:(j + 1) * LANES]

    nt = (((1,), (1,)), ((), ()))
    wr = wr_ref[...]
    wr_hi = wr.astype(BF16)
    wr_lo = (wr - wr_hi.astype(F32)).astype(BF16)
    h2_hi = h2.astype(BF16)
    h2_lo = (h2 - h2_hi.astype(F32)).astype(BF16)
    logits = (lax.dot_general(wr_hi, h2_hi, nt, preferred_element_type=F32)
              + lax.dot_general(wr_hi, h2_lo, nt, preferred_element_type=F32)
              + lax.dot_general(wr_lo, h2_hi, nt, preferred_element_type=F32)
              + br_ref[...])
    e_iota = lax.broadcasted_iota(I32, logits.shape, 0)
    vals = []
    chosen = jnp.zeros(logits.shape, F32)
    for r in range(TOP_K):
        m = jnp.max(logits, axis=0, keepdims=True)
        ix = jnp.min(jnp.where(logits == m, e_iota, N_EXPERTS), axis=0, keepdims=True)
        vals.append(m)
        idx_ref[r:r + 1, :] = ix
        hit = e_iota == ix
        chosen = chosen + jnp.where(hit, 1.0, 0.0)
        logits = jnp.where(hit, -jnp.inf, logits)

    @pl.when(pl.program_id(0) == 0)
    def _():
        cnt_ref[...] = jnp.zeros(cnt_ref.shape, F32)

    cnt_ref[...] = cnt_ref[...] + jnp.sum(chosen, axis=1, keepdims=True)
    ex = [jnp.exp(v - vals[0]) for v in vals]
    den = ex[0] + ex[1] + ex[2] + ex[3]
    for r in range(TOP_K):
        gate_ref[r:r + 1, :] = ex[r] / den


def _out_proj(d_out, r_out, w_o_b, x2d, g2, w_router_t, b_router):
    n, d = x2d.shape
    tm = OUT_TM
    const = lambda i: (0, 0)
    return pl.pallas_call(
        _outproj_kernel,
        grid=(n // tm,),
        in_specs=[pl.BlockSpec((tm, DIFF_WIDTH), lambda i: (i, 0)),
                  pl.BlockSpec((tm, RET_WIDTH), lambda i: (i, 0)),
                  pl.BlockSpec((DIFF_WIDTH + RET_WIDTH, d), const),
                  pl.BlockSpec((tm, d), lambda i: (i, 0)),
                  pl.BlockSpec((1, d), const),
                  pl.BlockSpec((N_EXPERTS, d), const),
                  pl.BlockSpec((N_EXPERTS, 1), const)],
        out_specs=[pl.BlockSpec((tm, d), lambda i: (i, 0)),
                   pl.BlockSpec((tm * (d // 2 // LANES), LANES), lambda i: (i, 0)),
                   pl.BlockSpec((TOP_K, tm), lambda i: (0, i)),
                   pl.BlockSpec((TOP_K, tm), lambda i: (0, i)),
                   pl.BlockSpec((N_EXPERTS, 1), const)],
        out_shape=[jax.ShapeDtypeStruct((n, d), F32),
                   jax.ShapeDtypeStruct((n * (d // 2 // LANES), LANES), U32),
                   jax.ShapeDtypeStruct((TOP_K, n), I32),
                   jax.ShapeDtypeStruct((TOP_K, n), F32),
                   jax.ShapeDtypeStruct((N_EXPERTS, 1), F32)],
        compiler_params=_cparams("arbitrary"),
        name="out_proj",
    )(d_out, r_out, w_o_b, x2d, g2.reshape(1, d), w_router_t, b_router.reshape(N_EXPERTS, 1))


def _plan_kernel(idx_ref, cnt_ref, dest_ref, blk_ref, ends_ref, base_sc, tri_sc, *, nb_pad):
    c = pl.program_id(0)
    tc = PLAN_TC
    e_iota = lax.broadcasted_iota(I32, (N_EXPERTS, tc), 0)

    @pl.when(c == 0)
    def _():
        s = lax.broadcasted_iota(I32, (tc, tc), 0)
        t = lax.broadcasted_iota(I32, (tc, tc), 1)
        tri_sc[...] = jnp.where(s < t, 1.0, 0.0).astype(BF16)
        cnt = cnt_ref[...]
        nblk = jnp.floor((cnt + (MOE_BM - 1.0)) * (1.0 / MOE_BM))
        ei = lax.broadcasted_iota(I32, (N_EXPERTS, LANES), 0)
        li = lax.broadcasted_iota(I32, (N_EXPERTS, LANES), 1)
        nblk_row = jnp.sum(jnp.where(ei == li, nblk, 0.0), axis=0, keepdims=True)
        start = jnp.sum(jnp.where(li < ei, nblk_row, 0.0), axis=1, keepdims=True)
        base_sc[...] = start * MOE_BM
        end = start + nblk
        bi = lax.broadcasted_iota(I32, (N_EXPERTS, nb_pad), 1).astype(F32)
        be = jnp.sum(jnp.where(end <= bi, 1.0, 0.0), axis=0, keepdims=True)
        blk_ref[...] = jnp.minimum(be, N_EXPERTS - 1.0).astype(I32)
        ends_ref[...] = jnp.sum(jnp.where(ei == li, end, 0.0), axis=0, keepdims=True).astype(I32)

    base = base_sc[...]
    for k in range(TOP_K):
        oh = idx_ref[k:k + 1, :] == e_iota
        ohb = jnp.where(oh, 1.0, 0.0).astype(BF16)
        before = jnp.dot(ohb, tri_sc[...], preferred_element_type=F32)
        rank = jnp.sum(jnp.where(oh, before + base, 0.0), axis=0, keepdims=True).astype(I32)
        for g in range(tc // LANES):
            dest_ref[g * TOP_K + k:g * TOP_K + k + 1, :] = rank[:, g * LANES:(g + 1) * LANES]
        base = base + jnp.sum(jnp.where(oh, 1.0, 0.0), axis=1, keepdims=True)
    base_sc[...] = base


def _plan(top_idx, counts, nb_pad):
    n = top_idx.shape[1]
    tc = PLAN_TC
    return pl.pallas_call(
        functools.partial(_plan_kernel, nb_pad=nb_pad),
        grid=(n // tc,),
        in_specs=[pl.BlockSpec((TOP_K, tc), lambda c: (0, c)),
                  pl.BlockSpec((N_EXPERTS, 1), lambda c: (0, 0))],
        out_specs=[pl.BlockSpec((tc // LANES * TOP_K, LANES), lambda c: (c, 0)),
                   pl.BlockSpec((1, nb_pad), lambda c: (0, 0)),
                   pl.BlockSpec((1, LANES), lambda c: (0, 0))],
        out_shape=[jax.ShapeDtypeStruct((n // LANES * TOP_K, LANES), I32),
                   jax.ShapeDtypeStruct((1, nb_pad), I32),
                   jax.ShapeDtypeStruct((1, LANES), I32)],
        scratch_shapes=[pltpu.VMEM((N_EXPERTS, 1), F32),
                        pltpu.VMEM((tc, tc), BF16)],
        compiler_params=_cparams("arbitrary"),
        name="plan",
    )(top_idx, counts)


def _dispatch_kernel(ends_ref, dest_ref, h_ref, xbuf_ref, zbuf, sem, zsem, *, nb):
    tm = DISP_TM
    bm = MOE_BM

    @pl.when(pl.program_id(0) == 0)
    def _():
        zbuf[...] = jnp.zeros(zbuf.shape, zbuf.dtype)

        def zero_block(blk):
            return pltpu.make_async_copy(zbuf, xbuf_ref.at[pl.ds(pl.multiple_of(blk * bm, bm), bm)], zsem)

        def per_block(fn):
            for e in range(N_EXPERTS):
                end = ends_ref[e]
                first = ends_ref[e - 1] if e else 0

                @pl.when(end > first)
                def _():
                    fn(zero_block(end - 1))

            def tail(blk, carry):
                fn(zero_block(blk))
                return carry

            lax.fori_loop(ends_ref[N_EXPERTS - 1], nb, tail, 0)

        per_block(lambda cp: cp.start())
        per_block(lambda cp: cp.wait())

    def issue(g, carry):
        for u in range(LANES):
            for k in range(TOP_K):
                src = h_ref.at[g * LANES + u]
                dst = xbuf_ref.at[dest_ref[g * TOP_K + k, u]]
                pltpu.make_async_copy(src, dst, sem).start(priority=k % 2)
        return carry

    lax.fori_loop(0, tm // LANES, issue, 0)
    for k in range(TOP_K):
        pltpu.make_async_copy(xbuf_ref.at[pl.ds(0, tm)], xbuf_ref.at[pl.ds(0, tm)], sem).wait()


def _dispatch(ends, dest, h2p, p_rows):
    n, s, w = h2p.shape
    tm = DISP_TM
    grid_spec = pltpu.PrefetchScalarGridSpec(
        num_scalar_prefetch=1,
        grid=(n // tm,),
        in_specs=[pl.BlockSpec((tm // LANES * TOP_K, LANES), lambda i, ends: (i, 0), memory_space=pltpu.SMEM),
                  pl.BlockSpec((tm, s, w), lambda i, ends: (i, 0, 0))],
        out_specs=pl.BlockSpec(memory_space=pl.ANY),
        scratch_shapes=[pltpu.VMEM((MOE_BM, s, w), h2p.dtype),
                        pltpu.SemaphoreType.DMA(()),
                        pltpu.SemaphoreType.DMA(())],
    )
    return pl.pallas_call(
        functools.partial(_dispatch_kernel, nb=p_rows // MOE_BM),
        grid_spec=grid_spec,
        out_shape=jax.ShapeDtypeStruct((p_rows, s, w), h2p.dtype),
        compiler_params=pltpu.CompilerParams(dimension_semantics=("arbitrary",),
                                             vmem_limit_bytes=VMEM_LIMIT,
                                             has_side_effects=True),
        name="dispatch",
    )(ends, dest, h2p)


def _pair_perm():
    a = lax.broadcasted_iota(I32, (MXU_DIM, MXU_DIM), 0)
    b = lax.broadcasted_iota(I32, (MXU_DIM, MXU_DIM), 1)
    src = jnp.where(b < LANES, 2 * b, 2 * (b - LANES) + 1)
    return jnp.where(a == src, 1.0, 0.0).astype(BF16)


def _pair_group(b):
    e, f2 = b.shape
    return b.reshape(e, f2 // MXU_DIM, LANES, 2).transpose(0, 1, 3, 2).reshape(e, 1, f2)


def _expert_kernel(blk_ref, ends_ref, x_ref, w1_hbm, b1_ref, w2_hbm, b2_ref, y_ref,
                   w1f, w2f, w1_ref, w2_ref, wsem, slot_ref):
    b = pl.program_id(0)
    n_used = ends_ref[N_EXPERTS - 1]

    @pl.when(b >= n_used)
    def _():
        zero = jnp.zeros(y_ref.shape, F32)
        y_ref[...] = pltpu.pack_elementwise([zero, zero], packed_dtype=BF16)

    def fetch(expert, slot):
        return (pltpu.make_async_copy(w1_hbm.at[expert], w1f.at[slot], wsem.at[slot]),
                pltpu.make_async_copy(w2_hbm.at[expert], w2f.at[slot], wsem.at[slot]))

    @pl.when(b == 0)
    def _():
        slot_ref[0] = 0
        for cp in fetch(blk_ref[0], 0):
            cp.start()

    new_expert = (b == 0) | (blk_ref[b] != blk_ref[jnp.maximum(b - 1, 0)])

    @pl.when((b < n_used) & new_expert)
    def _():
        slot = slot_ref[0]
        expert = blk_ref[b]
        for cp in fetch(expert, slot):
            cp.wait()
        next_first = ends_ref[expert]

        @pl.when(next_first < n_used)
        def _():
            for cp in fetch(blk_ref[next_first], 1 - slot):
                cp.start()

        perm = _pair_perm()
        for c in range(w1f.shape[2] // MXU_DIM):
            cols = slice(c * MXU_DIM, (c + 1) * MXU_DIM)
            blk = w1f[slot, :, cols].astype(BF16)
            w1_ref[0, :, cols] = jnp.dot(blk, perm, preferred_element_type=F32).astype(BF16)
        w2_ref[0] = w2f[slot].astype(BF16)
        slot_ref[0] = 1 - slot

    @pl.when(b < n_used)
    def _():
        ns = w2_ref.shape[2] // 2 // LANES
        bm = x_ref.shape[0] // ns
        slabs = [x_ref[pl.ds(j, bm, stride=ns), :] for j in range(ns)]
        lo = [pltpu.unpack_elementwise(w, index=0, packed_dtype=BF16, unpacked_dtype=F32).astype(BF16)
              for w in slabs]
        hi = [pltpu.unpack_elementwise(w, index=1, packed_dtype=BF16, unpacked_dtype=F32).astype(BF16)
              for w in slabs]
        x = jnp.concatenate(lo + hi, axis=1)
        acc = jnp.zeros((x.shape[0], w2_ref.shape[2]), F32)
        grp = 2 * MXU_DIM
        for c in range(w1_ref.shape[2] // grp):
            cols = slice(c * grp, (c + 1) * grp)
            h = jnp.dot(x, w1_ref[0, :, cols], preferred_element_type=F32) + b1_ref[0, :, cols]
            glu = jnp.concatenate([h[:, 0:LANES], h[:, 2 * LANES:3 * LANES]], axis=1)
            lin = jnp.concatenate([h[:, LANES:2 * LANES], h[:, 3 * LANES:4 * LANES]], axis=1)
            glu = jnp.minimum(glu, SWIGLU_LIMIT)
            lin = jnp.clip(lin, -SWIGLU_LIMIT, SWIGLU_LIMIT)
            act = glu * jax.nn.sigmoid(SWIGLU_ALPHA * glu) * (lin + 1.0)
            acc = acc + jnp.dot(act.astype(BF16), w2_ref[0, c * MXU_DIM:(c + 1) * MXU_DIM, :],
                                preferred_element_type=F32)
        y = acc + b2_ref[0]
        half = y.shape[1] // 2
        packed = pltpu.pack_elementwise([y[:, :half], y[:, half:]], packed_dtype=BF16)
        for j in range(ns):
            y_ref[pl.ds(j, bm, stride=ns), :] = packed[:, j * LANES:(j + 1) * LANES]


def _experts(blk_expert, ends, x_buf, w1, b1, w2, b2):
    e, d, f2 = w1.shape
    xs = d // 2 // LANES
    ys = xs
    p_rows = x_buf.shape[0] // xs
    f = f2 // 2
    bm = MOE_BM
    nb = p_rows // bm

    def row_blk(b, ends):
        return jnp.minimum(b, ends[N_EXPERTS - 1] - 1)

    grid_spec = pltpu.PrefetchScalarGridSpec(
        num_scalar_prefetch=2,
        grid=(nb,),
        in_specs=[pl.BlockSpec((bm * xs, LANES), lambda b, blk, ends: (row_blk(b, ends), 0)),
                  pl.BlockSpec(memory_space=pl.ANY),
                  pl.BlockSpec((1, 1, f2), lambda b, blk, ends: (blk[row_blk(b, ends)], 0, 0)),
                  pl.BlockSpec(memory_space=pl.ANY),
                  pl.BlockSpec((1, 1, d), lambda b, blk, ends: (blk[row_blk(b, ends)], 0, 0))],
        out_specs=pl.BlockSpec((bm * ys, LANES), lambda b, blk, ends: (b, 0)),
        scratch_shapes=[pltpu.VMEM((2, d, f2), F32),
                        pltpu.VMEM((2, f, d), F32),
                        pltpu.VMEM((1, d, f2), BF16),
                        pltpu.VMEM((1, f, d), BF16),
                        pltpu.SemaphoreType.DMA((2,)),
                        pltpu.SMEM((1,), I32)],
    )
    return pl.pallas_call(
        _expert_kernel,
        grid_spec=grid_spec,
        out_shape=jax.ShapeDtypeStruct((p_rows * ys, LANES), x_buf.dtype),
        compiler_params=_cparams("arbitrary"),
        name="experts",
    )(blk_expert, ends, x_buf, w1, b1, w2, b2)


def _combine_kernel(dest_ref, next_dest_ref, gate_ref, x1_ref, gf_ref, ybuf_ref, o_ref, gbuf, sem):
    tm = COMB_TM
    i = pl.program_id(0)
    cur = i % 2

    ys = ybuf_ref.shape[1]

    def gather(idx_ref, buf):
        def issue(g, carry):
            for u in range(LANES):
                for k in range(TOP_K):
                    src = ybuf_ref.at[idx_ref[g * TOP_K + k, u]]
                    row = pl.multiple_of(g * (LANES * ys) + u * ys, ys)
                    dst = gbuf.at[buf, k, pl.ds(row, ys), :]
                    pltpu.make_async_copy(src, dst, sem.at[buf]).start(priority=k % 2)
            return carry

        lax.fori_loop(0, tm // LANES, issue, 0)

    @pl.when(i == 0)
    def _():
        gather(dest_ref, 0)

    @pl.when(i + 1 < pl.num_programs(0))
    def _():
        gather(next_dest_ref, 1 - cur)

    for k in range(TOP_K):
        pltpu.make_async_copy(ybuf_ref.at[pl.ds(0, tm)], ybuf_ref.at[pl.ds(0, tm)], sem.at[cur]).wait()

    gates = gate_ref[...]
    pad = jnp.zeros((LANES - TOP_K, LANES), F32)
    cols = []
    for c in range(tm // LANES):
        blk = jnp.concatenate([gates[:, c * LANES:(c + 1) * LANES], pad], axis=0)
        cols.append(blk.T)
    gcol = jnp.concatenate(cols, axis=0)
    for c in range(tm // COMB_CH):
        rows = slice(c * COMB_CH, (c + 1) * COMB_CH)
        gk = [gcol[rows, k:k + 1] for k in range(TOP_K)]
        sq = jnp.zeros((COMB_CH, LANES), F32)
        for j in range(ys):
            words = [gbuf[cur, k, pl.ds(c * COMB_CH * ys + j, COMB_CH, stride=ys), :] for k in range(TOP_K)]
            for part in range(2):
                col = (part * ys + j) * LANES
                a = x1_ref[rows, col:col + LANES]
                for k in range(TOP_K):
                    yk = pltpu.unpack_elementwise(words[k], index=part, packed_dtype=BF16, unpacked_dtype=F32)
                    a = a + yk * gk[k]
                sq = sq + a * a
                o_ref[rows, col:col + LANES] = a
        ms = jnp.sum(sq, axis=-1, keepdims=True) * (1.0 / o_ref.shape[1])
        o_ref[rows, :] = o_ref[rows, :] * lax.rsqrt(ms + RMS_EPS) * gf_ref[...]


def _combine(dest, gates, x1, gf, y_buf):
    n, d = x1.shape
    tm = COMB_TM
    last = n // tm - 1
    idx_rows = tm // LANES * TOP_K
    return pl.pallas_call(
        _combine_kernel,
        grid=(n // tm,),
        in_specs=[pl.BlockSpec((idx_rows, LANES), lambda i: (i, 0), memory_space=pltpu.SMEM),
                  pl.BlockSpec((idx_rows, LANES), lambda i: (jnp.minimum(i + 1, last), 0), memory_space=pltpu.SMEM),
                  pl.BlockSpec((TOP_K, tm), lambda i: (0, i)),
                  pl.BlockSpec((tm, d), lambda i: (i, 0)),
                  pl.BlockSpec((1, d), lambda i: (0, 0)),
                  pl.BlockSpec(memory_space=pl.ANY)],
        out_specs=pl.BlockSpec((tm, d), lambda i: (i, 0)),
        out_shape=jax.ShapeDtypeStruct((n, d), F32),
        scratch_shapes=[pltpu.VMEM((2, TOP_K, tm * y_buf.shape[1], LANES), y_buf.dtype),
                        pltpu.SemaphoreType.DMA((2,))],
        compiler_params=_cparams("arbitrary"),
        name="combine",
    )(dest, dest, gates, x1, gf.reshape(1, d), y_buf)


def kernel(x, positions, norm1_g, w_in, lambda_q1, lambda_k1, lambda_q2, lambda_k2, diff_norm_g, ret_norm_g, w_o, norm2_g, w_router, b_router, w_moe_in, b_moe_in, w_moe_out, b_moe_out, norm_f_g):
    batch, seq, d = x.shape
    n = batch * seq
    assert norm1_g.shape[0] == 1, "single-layer block"
    l = 0
    p_rows = n * TOP_K + N_EXPERTS * MOE_BM
    nb = p_rows // MOE_BM
    nb_pad = -(-nb // LANES) * LANES

    x2d = x.reshape(n, d)
    lam_init = 0.8 - 0.6 * math.exp(-0.3 * l)
    proj = _in_proj(x2d, norm1_g[l], w_in[l].astype(BF16), positions)
    d_out = _diff_attention(proj, lambda_q1[l], lambda_k1[l], lambda_q2[l], lambda_k2[l],
                            diff_norm_g[l], batch, seq, lam_init)
    r_out = _retention(proj, ret_norm_g[l], batch, seq)
    x1, h2p, top_idx, gates, counts = _out_proj(d_out, r_out, w_o[l].astype(BF16), x2d, norm2_g[l],
                                                w_router[l].T, b_router[l])
    dest, blk_expert, ends = _plan(top_idx, counts, nb_pad)
    ends = ends.reshape(-1)
    xs = ys = d // 2 // LANES
    x_buf = _dispatch(ends, dest, h2p.reshape(n, xs, LANES), p_rows)
    b1 = _pair_group(b_moe_in[l])
    b2 = b_moe_out[l].reshape(N_EXPERTS, 1, -1)
    y_buf = _experts(blk_expert.reshape(-1), ends, x_buf.reshape(p_rows * xs, LANES),
                     w_moe_in[l], b1, w_moe_out[l], b2)
    out = _combine(dest, gates, x1, norm_f_g, y_buf.reshape(p_rows, ys, LANES))
    return out.reshape(batch, seq, d)
```
